```python
import math
import jax, jax.numpy as jnp
from jax import lax
import numpy as np

D_MODEL = 1024
BATCH = 8
SEQ = 2048
DEPTH = 4
DEC_BATCH = 32
DEC_SEQ = 4
PAST_LEN = 8192
PAGE_SIZE = 128

N_MIXERS = 4
H_DIFF = 8
DH_DIFF = 64
H_SB = 16
DH_SB = D_MODEL // H_SB
N_BUCKETS = 32
MAX_DISTANCE = 128
POOL_WINDOWS = (2, 4, 8, 16)
POOL_GROUP = D_MODEL // len(POOL_WINDOWS)
POOL_HIST = max(POOL_WINDOWS) - 1
CONV_WIDTH = 31
D_FF = 128 * math.ceil(8 * D_MODEL / 3 / 128)
FFN_CONV_WIDTH = 3
Q_BLOCK = 128
EPS = 1e-6
NEG_INF = -1e30

kernel_name = 'hybrid_diff_stickbreak_pool_conformer_step'


def rms_norm(x, g):
    xf = x.astype(jnp.float32)
    y = xf * lax.rsqrt(jnp.mean(xf * xf, axis=-1, keepdims=True) + EPS)
    return (y * g.astype(jnp.float32)).astype(x.dtype)


def layer_norm(x, g, b):
    xf = x.astype(jnp.float32)
    mu = jnp.mean(xf, axis=-1, keepdims=True)
    xc = xf - mu
    var = jnp.mean(xc * xc, axis=-1, keepdims=True)
    y = xc * lax.rsqrt(var + EPS) * g.astype(jnp.float32) + b.astype(jnp.float32)
    return y.astype(x.dtype)


def causal_depthwise_conv(z, w):
    c = z.shape[-1]
    return lax.conv_general_dilated(z, w.astype(z.dtype)[:, None, :], (1,), 'VALID',
                                    dimension_numbers=('NWC', 'WIO', 'NWC'),
                                    feature_group_count=c)


def gather_pages(cache, page_table):
    g = cache[page_table]
    return g.reshape(g.shape[0], g.shape[1] * g.shape[2], *g.shape[3:])


def t5_bucket(rel):
    n = jnp.maximum(rel, 0)
    max_exact = N_BUCKETS // 2
    nf = jnp.maximum(n, 1).astype(jnp.float32)
    large = max_exact + (jnp.log(nf / max_exact) / math.log(MAX_DISTANCE / max_exact)
                         * (N_BUCKETS - max_exact)).astype(jnp.int32)
    large = jnp.minimum(large, N_BUCKETS - 1)
    return jnp.where(n < max_exact, n, large)


def sweep_query_blocks(fn, q):
    b, s = q.shape[:2]
    nb = s // Q_BLOCK
    qb = jnp.moveaxis(q.reshape(b, nb, Q_BLOCK, *q.shape[2:]), 1, 0)
    starts = jnp.arange(nb, dtype=jnp.int32) * Q_BLOCK
    out = lax.map(lambda args: fn(args[0], args[1]), (qb, starts))
    return jnp.moveaxis(out, 0, 1).reshape(b, s, *out.shape[3:])


def diff_project(u, w_qkv):
    b, l, _ = u.shape
    q, k, v = jnp.split(u @ w_qkv, 3, axis=-1)
    return (q.reshape(b, l, H_DIFF, 2, DH_DIFF),
            k.reshape(b, l, H_DIFF, 2 * DH_DIFF),
            v.reshape(b, l, H_DIFF, 2 * DH_DIFF))


def diff_lambda(lq1, lk1, lq2, lk2, lam_init):
    e1 = jnp.exp(jnp.sum(lq1.astype(jnp.float32) * lk1.astype(jnp.float32)))
    e2 = jnp.exp(jnp.sum(lq2.astype(jnp.float32) * lk2.astype(jnp.float32)))
    return e1 - e2 + lam_init


def diff_attend(q, k, v, q_pos, k_pos, rel_bias, lam):
    b, lk = k.shape[:2]
    k = k.reshape(b, lk, H_DIFF, 2, DH_DIFF)
    s = jnp.einsum('bqhcd,bkhcd->bhcqk', q, k).astype(jnp.float32) * (DH_DIFF ** -0.5)
    rel = q_pos[:, None] - k_pos[None, :]
    bias = jnp.moveaxis(rel_bias[t5_bucket(rel)], -1, 0).astype(jnp.float32)
    s = jnp.where(rel >= 0, s + bias[None, :, None], NEG_INF)
    p = jax.nn.softmax(s, axis=-1)
    w = p[:, :, 0] - lam * p[:, :, 1]
    return jnp.einsum('bhqk,bkhe->bqhe', w.astype(v.dtype), v)


def diff_merge(o, subln, lam_init, w_o):
    b, l = o.shape[:2]
    o = rms_norm(o, subln) * (1.0 - lam_init)
    return o.reshape(b, l, H_DIFF * 2 * DH_DIFF) @ w_o


def sb_project(u, w_qkv):
    b, l, _ = u.shape
    q, k, v = jnp.split(u @ w_qkv, 3, axis=-1)
    shp = (b, l, H_SB, DH_SB)
    return q.reshape(shp), k.reshape(shp), v.reshape(shp)


def sb_attend(q, k, v, q_pos, k_pos):
    z = jnp.einsum('bqhd,bkhd->bhqk', q, k).astype(jnp.float32) * (DH_SB ** -0.5)
    mask = k_pos[None, :] < q_pos[:, None]
    log_keep = jnp.where(mask, jax.nn.log_sigmoid(-z), 0.0)
    log_tail = lax.cumsum(log_keep, axis=3, reverse=True) - log_keep
    a = jnp.where(mask, jnp.exp(jax.nn.log_sigmoid(z) + log_tail), 0.0)
    return jnp.einsum('bhqk,bkhd->bqhd', a.astype(v.dtype), v)


def sb_merge(o, w_o):
    b, l = o.shape[:2]
    return o.reshape(b, l, H_SB * DH_SB) @ w_o


def pool_mix(u, hist, pos0, w_pool, pool_scale):
    b, l, _ = u.shape
    z = jnp.concatenate([hist.astype(u.dtype), u], axis=1)
    zf = z.astype(jnp.float32)
    csum = jnp.pad(jnp.cumsum(zf, axis=1), ((0, 0), (1, 0), (0, 0)))
    pos = pos0 + jnp.arange(l, dtype=jnp.int32)
    parts = []
    for g, w in enumerate(POOL_WINDOWS):
        c0, c1 = g * POOL_GROUP, (g + 1) * POOL_GROUP
        start = POOL_HIST + 1 - w
        win_sum = csum[:, POOL_HIST + 1:, c0:c1] - csum[:, start:start + l, c0:c1]
        cnt = jnp.minimum(pos + 1, w).astype(jnp.float32)[None, :, None]
        parts.append(win_sum / cnt - zf[:, POOL_HIST:, c0:c1])
    d = jnp.stack(parts, axis=2).astype(u.dtype)
    y = jnp.einsum('blgc,gce->blge', d, w_pool).reshape(b, l, D_MODEL) * pool_scale
    return y, z[:, -POOL_HIST:]


def conformer_conv(u, hist, w_pw1, b_pw1, w_dw, b_dw, ln_g, ln_b, w_pw2, b_pw2):
    a, g = jnp.split(u @ w_pw1 + b_pw1, 2, axis=-1)
    glu = a * jax.nn.sigmoid(g)
    z = jnp.concatenate([hist.astype(glu.dtype), glu], axis=1)
    c = causal_depthwise_conv(z, w_dw) + b_dw
    c = jax.nn.silu(layer_norm(c, ln_g, ln_b))
    return c @ w_pw2 + b_pw2, z[:, -(CONV_WIDTH - 1):]


def conv_ffn(h, hist, w_up, w_dw, w_down):
    up = h @ w_up
    z = jnp.concatenate([hist.astype(up.dtype), up], axis=1)
    c = causal_depthwise_conv(z, w_dw)
    gate, val = jnp.split(c, 2, axis=-1)
    return (jax.nn.silu(gate) * val) @ w_down, z[:, -(FFN_CONV_WIDTH - 1):]


def setup_inputs(seed: int = 0) -> dict:
    key = jax.random.key(seed)
    keys = iter(jax.random.split(key, 48))

    def nrm(shape, scale=1.0):
        return jax.random.normal(next(keys), shape, jnp.float32) * scale

    def gain(shape):
        return 1.0 + nrm(shape, 0.05)

    n_pages = PAST_LEN // PAGE_SIZE
    n_used = DEC_BATCH * n_pages
    n_phys = n_used + max(1, n_used // 4)
    page_table = jax.random.permutation(next(keys), n_phys)[:n_used].reshape(
        DEC_BATCH, n_pages).astype(jnp.int32)
    return {
        'x_prompt': nrm((BATCH, SEQ, D_MODEL)),
        'x_sample': nrm((DEC_BATCH, DEC_SEQ, D_MODEL)),
        'cache_k_diff': nrm((n_phys, PAGE_SIZE, H_DIFF, 2 * DH_DIFF)),
        'cache_v_diff': nrm((n_phys, PAGE_SIZE, H_DIFF, 2 * DH_DIFF)),
        'cache_k_sb': nrm((n_phys, PAGE_SIZE, H_SB, DH_SB)),
        'cache_v_sb': nrm((n_phys, PAGE_SIZE, H_SB, DH_SB)),
        'state_pool': nrm((DEC_BATCH, POOL_HIST, D_MODEL)),
        'state_conv': nrm((DEC_BATCH, CONV_WIDTH - 1, D_MODEL), 0.5),
        'state_ffn': nrm((DEPTH, DEC_BATCH, FFN_CONV_WIDTH - 1, 2 * D_FF)),
        'page_table': page_table,
        'rel_bias': nrm((N_BUCKETS, H_DIFF), 0.5),
        'norm_mix': gain((DEPTH, D_MODEL)),
        'norm_ffn': gain((DEPTH, D_MODEL)),
        'norm_final': gain((D_MODEL,)),
        'diff_w_qkv': nrm((D_MODEL, 3 * H_DIFF * 2 * DH_DIFF), D_MODEL ** -0.5),
        'diff_w_o': nrm((H_DIFF * 2 * DH_DIFF, D_MODEL), (H_DIFF * 2 * DH_DIFF) ** -0.5),
        'diff_lambda_q1': nrm((DH_DIFF,), 0.1),
        'diff_lambda_k1': nrm((DH_DIFF,), 0.1),
        'diff_lambda_q2': nrm((DH_DIFF,), 0.1),
        'diff_lambda_k2': nrm((DH_DIFF,), 0.1),
        'diff_subln': gain((2 * DH_DIFF,)),
        'sb_w_qkv': nrm((D_MODEL, 3 * H_SB * DH_SB), D_MODEL ** -0.5),
        'sb_w_o': nrm((H_SB * DH_SB, D_MODEL), (H_SB * DH_SB) ** -0.5),
        'pool_w': nrm((len(POOL_WINDOWS), POOL_GROUP, POOL_GROUP), POOL_GROUP ** -0.5),
        'pool_scale': gain((D_MODEL,)),
        'conv_w_pw1': nrm((D_MODEL, 2 * D_MODEL), D_MODEL ** -0.5),
        'conv_b_pw1': nrm((2 * D_MODEL,), 0.02),
        'conv_w_dw': nrm((CONV_WIDTH, D_MODEL), CONV_WIDTH ** -0.5),
        'conv_b_dw': nrm((D_MODEL,), 0.02),
        'conv_ln_g': gain((D_MODEL,)),
        'conv_ln_b': nrm((D_MODEL,), 0.02),
        'conv_w_pw2': nrm((D_MODEL, D_MODEL), D_MODEL ** -0.5),
        'conv_b_pw2': nrm((D_MODEL,), 0.02),
        'ffn_w_up': nrm((DEPTH, D_MODEL, 2 * D_FF), D_MODEL ** -0.5),
        'ffn_w_dw': nrm((DEPTH, FFN_CONV_WIDTH, 2 * D_FF), FFN_CONV_WIDTH ** -0.5),
        'ffn_w_down': nrm((DEPTH, D_FF, D_MODEL), D_FF ** -0.5),
    }


def reference(x_prompt, x_sample, cache_k_diff, cache_v_diff, cache_k_sb, cache_v_sb,
              state_pool, state_conv, state_ffn, page_table,
              rel_bias, norm_mix, norm_ffn, norm_final,
              diff_w_qkv, diff_w_o, diff_lambda_q1, diff_lambda_k1, diff_lambda_q2,
              diff_lambda_k2, diff_subln,
              sb_w_qkv, sb_w_o,
              pool_w, pool_scale,
              conv_w_pw1, conv_b_pw1, conv_w_dw, conv_b_dw, conv_ln_g, conv_ln_b,
              conv_w_pw2, conv_b_pw2,
              ffn_w_up, ffn_w_dw, ffn_w_down):
    bp, lp = x_prompt.shape[:2]
    bs, ls = x_sample.shape[:2]
    past = page_table.shape[1] * cache_k_diff.shape[1]
    pos_p = jnp.arange(lp, dtype=jnp.int32)
    pos_s = past + jnp.arange(ls, dtype=jnp.int32)
    pos_ctx = jnp.arange(past + ls, dtype=jnp.int32)
    q_off = jnp.arange(Q_BLOCK, dtype=jnp.int32)

    xp, xs = x_prompt, x_sample
    ffn_new_p, ffn_new_s = [], []
    for layer in range(DEPTH):
        kind = layer % N_MIXERS
        up = rms_norm(xp, norm_mix[layer])
        us = rms_norm(xs, norm_mix[layer])
        if kind == 0:
            lam_init = 0.8 - 0.6 * math.exp(-0.3 * layer)
            lam = diff_lambda(diff_lambda_q1, diff_lambda_k1, diff_lambda_q2, diff_lambda_k2, lam_init)
            qp, kp, vp = diff_project(up, diff_w_qkv)
            qs, ks, vs = diff_project(us, diff_w_qkv)
            op = sweep_query_blocks(
                lambda qb, st: diff_attend(qb, kp, vp, st + q_off, pos_p, rel_bias, lam), qp)
            k_ctx = jnp.concatenate([gather_pages(cache_k_diff, page_table).astype(ks.dtype), ks], axis=1)
            v_ctx = jnp.concatenate([gather_pages(cache_v_diff, page_table).astype(vs.dtype), vs], axis=1)
            os_ = diff_attend(qs, k_ctx, v_ctx, pos_s, pos_ctx, rel_bias, lam)
            yp = diff_merge(op, diff_subln, lam_init, diff_w_o)
            ys = diff_merge(os_, diff_subln, lam_init, diff_w_o)
            k_diff_p, v_diff_p, k_diff_s, v_diff_s = kp, vp, ks, vs
        elif kind == 1:
            qp, kp, vp = sb_project(up, sb_w_qkv)
            qs, ks, vs = sb_project(us, sb_w_qkv)
            op = sweep_query_blocks(
                lambda qb, st: sb_attend(qb, kp, vp, st + q_off, pos_p), qp)
            k_ctx = jnp.concatenate([gather_pages(cache_k_sb, page_table).astype(ks.dtype), ks], axis=1)
            v_ctx = jnp.concatenate([gather_pages(cache_v_sb, page_table).astype(vs.dtype), vs], axis=1)
            os_ = sb_attend(qs, k_ctx, v_ctx, pos_s, pos_ctx)
            yp = sb_merge(op, sb_w_o)
            ys = sb_merge(os_, sb_w_o)
            k_sb_p, v_sb_p, k_sb_s, v_sb_s = kp, vp, ks, vs
        elif kind == 2:
            yp, pool_p = pool_mix(up, jnp.zeros((bp, POOL_HIST, D_MODEL), up.dtype), 0,
                                  pool_w, pool_scale)
            ys, pool_s = pool_mix(us, state_pool, past, pool_w, pool_scale)
        else:
            yp, conv_p = conformer_conv(up, jnp.zeros((bp, CONV_WIDTH - 1, D_MODEL), up.dtype),
                                        conv_w_pw1, conv_b_pw1, conv_w_dw, conv_b_dw,
                                        conv_ln_g, conv_ln_b, conv_w_pw2, conv_b_pw2)
            ys, conv_s = conformer_conv(us, state_conv,
                                        conv_w_pw1, conv_b_pw1, conv_w_dw, conv_b_dw,
                                        conv_ln_g, conv_ln_b, conv_w_pw2, conv_b_pw2)
        xp = xp + yp
        xs = xs + ys
        hp = rms_norm(xp, norm_ffn[layer])
        hs = rms_norm(xs, norm_ffn[layer])
        fp, fstate_p = conv_ffn(hp, jnp.zeros((bp, FFN_CONV_WIDTH - 1, 2 * D_FF), hp.dtype),
                                ffn_w_up[layer], ffn_w_dw[layer], ffn_w_down[layer])
        fs, fstate_s = conv_ffn(hs, state_ffn[layer],
                                ffn_w_up[layer], ffn_w_dw[layer], ffn_w_down[layer])
        xp = xp + fp
        xs = xs + fs
        ffn_new_p.append(fstate_p)
        ffn_new_s.append(fstate_s)

    y_prompt = rms_norm(xp, norm_final)
    y_sample = rms_norm(xs, norm_final)
    ffn_prompt = jnp.stack(ffn_new_p, axis=0)
    ffn_sample = jnp.stack(ffn_new_s, axis=0)
    return (y_prompt, y_sample,
            k_diff_p, v_diff_p, k_sb_p, v_sb_p, pool_p, conv_p, ffn_prompt,
            k_diff_s, v_diff_s, k_sb_s, v_sb_s, pool_s, conv_s, ffn_sample)
```

```python
import functools
import math

import jax
import jax.numpy as jnp
from jax import lax
from jax.experimental import pallas as pl
from jax.experimental.pallas import tpu as pltpu

D_MODEL = 1024
H_DIFF = 8
DH_DIFF = 64
H_SB = 16
DH_SB = 64
N_BUCKETS = 32
MAX_DISTANCE = 128
POOL_WINDOWS = (2, 4, 8, 16)
POOL_GROUP = D_MODEL // len(POOL_WINDOWS)
POOL_HIST = max(POOL_WINDOWS) - 1
CONV_WIDTH = 31
FFN_CONV_WIDTH = 3
EPS = 1e-6
NEG_INF = -1e30

BF16 = jnp.bfloat16
F32 = jnp.float32

V7X_VMEM_BYTES = 64 * 1024 * 1024
VMEM_LIMIT = V7X_VMEM_BYTES - 12 * 1024 * 1024
SUBLANES = 8
LANES = 128

ATT_BLOCK = 128
FFN_CHUNK = 256


def _hist_pad(need, stride):
    h = need
    while (h * stride) % SUBLANES:
        h += 1
    return h

_NT = (((1,), (1,)), ((), ()))


def _params(*sem):
    return pltpu.CompilerParams(dimension_semantics=sem, vmem_limit_bytes=VMEM_LIMIT)


def _rms(x, g):
    return x * lax.rsqrt(jnp.mean(x * x, axis=-1, keepdims=True) + EPS) * g


def _row_tile(m, want):
    t = min(m, want)
    assert m % t == 0, (m, t)
    return t


def _qkv_kernel(x_ref, g_ref, w_ref, q16_ref, k32_ref, v32_ref, k16_ref, v16_ref):
    n = q16_ref.shape[-1]
    u = _rms(x_ref[...], g_ref[...]).astype(BF16)
    q = jnp.dot(u, w_ref[:, 0:n], preferred_element_type=F32)
    q16_ref[...] = q.astype(BF16)
    k = jnp.dot(u, w_ref[:, n:2 * n], preferred_element_type=F32)
    k32_ref[...] = k
    k16_ref[...] = k.astype(BF16)
    v = jnp.dot(u, w_ref[:, 2 * n:3 * n], preferred_element_type=F32)
    v32_ref[...] = v
    v16_ref[...] = v.astype(BF16)


def _qkv_proj(x2d, g, w16):
    m, d = x2d.shape
    n = w16.shape[1] // 3
    tm = _row_tile(m, 256)
    row = lambda i: (i, 0)
    const = lambda i: (0, 0)
    out_blk = pl.BlockSpec((tm, n), row)
    return pl.pallas_call(
        _qkv_kernel,
        grid=(m // tm,),
        in_specs=[pl.BlockSpec((tm, d), row), pl.BlockSpec((1, d), const),
                  pl.BlockSpec((d, 3 * n), const)],
        out_specs=[out_blk] * 5,
        out_shape=[jax.ShapeDtypeStruct((m, n), BF16), jax.ShapeDtypeStruct((m, n), F32),
                   jax.ShapeDtypeStruct((m, n), F32), jax.ShapeDtypeStruct((m, n), BF16),
                   jax.ShapeDtypeStruct((m, n), BF16)],
        compiler_params=_params("parallel"),
        name="qkv_proj",
    )(x2d, g.reshape(1, d), w16)


def _mm_resid_kernel(a_ref, w_ref, r_ref, o_ref):
    o_ref[...] = r_ref[...] + jnp.dot(a_ref[...].astype(BF16), w_ref[...],
                                      preferred_element_type=F32)


def _mm_resid(a2d, w16, resid2d):
    m, k = a2d.shape
    n = w16.shape[1]
    tm = _row_tile(m, 512)
    row = lambda i: (i, 0)
    return pl.pallas_call(
        _mm_resid_kernel,
        grid=(m // tm,),
        in_specs=[pl.BlockSpec((tm, k), row), pl.BlockSpec((k, n), lambda i: (0, 0)),
                  pl.BlockSpec((tm, n), row)],
        out_specs=pl.BlockSpec((tm, n), row),
        out_shape=jax.ShapeDtypeStruct((m, n), F32),
        compiler_params=_params("parallel"),
        name="out_proj_resid",
    )(a2d, w16, resid2d)


def _t5_bucket(rel):
    n = jnp.maximum(rel, 0)
    max_exact = N_BUCKETS // 2
    nf = jnp.maximum(n, 1).astype(F32)
    large = max_exact + (jnp.log(nf / max_exact) / math.log(MAX_DISTANCE / max_exact)
                         * (N_BUCKETS - max_exact)).astype(jnp.int32)
    large = jnp.minimum(large, N_BUCKETS - 1)
    return jnp.where(n < max_exact, n, large)


def _bias_table_kernel(rb_ref, o_ref):
    d = pl.program_id(0)
    shape = (ATT_BLOCK, ATT_BLOCK)
    rel = d * ATT_BLOCK + lax.broadcasted_iota(jnp.int32, shape, 0) \
        - lax.broadcasted_iota(jnp.int32, shape, 1)
    bucket = _t5_bucket(rel)
    for h in range(H_DIFF):
        val = jnp.zeros(shape, F32)
        for b in range(N_BUCKETS):
            val = jnp.where(bucket == b, rb_ref[b, h], val)
        o_ref[h] = jnp.where(rel >= 0, val, NEG_INF)


def _bias_table(rel_bias):
    return pl.pallas_call(
        _bias_table_kernel,
        grid=(3,),
        in_specs=[pl.BlockSpec(memory_space=pltpu.SMEM)],
        out_specs=pl.BlockSpec((None, H_DIFF, ATT_BLOCK, ATT_BLOCK), lambda d: (d, 0, 0, 0)),
        out_shape=jax.ShapeDtypeStruct((3, H_DIFF, ATT_BLOCK, ATT_BLOCK), F32),
        compiler_params=_params("parallel"),
        name="t5_bias_table",
    )(rel_bias)


def _diff_lambda(lq1, lk1, lq2, lk2, lam_init):
    e1 = jnp.exp(jnp.sum(lq1 * lk1, axis=-1, keepdims=True))
    e2 = jnp.exp(jnp.sum(lq2 * lk2, axis=-1, keepdims=True))
    return e1 - e2 + lam_init


def _diff_attn_kernel(q_ref, k_ref, v_ref, bias_ref, lq1_ref, lk1_ref, lq2_ref, lk2_ref,
                      subln_ref, o_ref, *, lam_init):
    blk = ATT_BLOCK
    nq = q_ref.shape[0] // blk
    first = lax.broadcasted_iota(jnp.int32, (blk, 2 * DH_DIFF), 1) < DH_DIFF
    lam = _diff_lambda(lq1_ref[...], lk1_ref[...], lq2_ref[...], lk2_ref[...], lam_init)
    subln = subln_ref[...] * (1.0 - lam_init)
    scale = DH_DIFF ** -0.5

    def q_block(qi, _):
        q = q_ref[pl.ds(pl.multiple_of(qi * blk, blk), blk), :] * scale
        zero = jnp.zeros_like(q)
        qa = jnp.where(first, q, zero)
        qb = jnp.where(first, zero, q)

        def kv_block(kj, carry):
            m1, l1, a1, m2, l2, a2 = carry
            start = pl.multiple_of(kj * blk, blk)
            ks = k_ref[pl.ds(start, blk), :]
            vs = v_ref[pl.ds(start, blk), :]
            bias = bias_ref[jnp.minimum(qi - kj, 2)]

            def update(qm, m, l, a):
                s = lax.dot_general(qm, ks, _NT, preferred_element_type=F32) + bias
                mn = jnp.maximum(m, jnp.max(s, axis=-1, keepdims=True))
                alpha = jnp.exp(m - mn)
                p = jnp.exp(s - mn)
                l = alpha * l + jnp.sum(p, axis=-1, keepdims=True)
                a = alpha * a + jnp.dot(p.astype(BF16), vs, preferred_element_type=F32)
                return mn, l, a

            m1, l1, a1 = update(qa, m1, l1, a1)
            m2, l2, a2 = update(qb, m2, l2, a2)
            return m1, l1, a1, m2, l2, a2

        col = jnp.zeros((blk, 1), F32)
        acc = jnp.zeros((blk, 2 * DH_DIFF), F32)
        init = (col + NEG_INF, col, acc, col + NEG_INF, col, acc)
        m1, l1, a1, m2, l2, a2 = lax.fori_loop(0, qi + 1, kv_block, init)
        o = a1 / l1 - lam * (a2 / l2)
        o = _rms(o, subln)
        o_ref[pl.ds(pl.multiple_of(qi * blk, blk), blk), :] = o.astype(o_ref.dtype)
        return 0

    lax.fori_loop(0, nq, q_block, 0)


def _diff_attn_prompt(q16, k16, v16, bias_tab, lq1, lk1, lq2, lk2, subln, lam_init):
    b, l, _ = q16.shape
    hd = 2 * DH_DIFF
    seq = pl.BlockSpec((None, l, hd), lambda i, h: (i, 0, h))
    vec = lambda n: pl.BlockSpec((1, n), lambda i, h: (0, 0))
    return pl.pallas_call(
        functools.partial(_diff_attn_kernel, lam_init=lam_init),
        grid=(b, H_DIFF),
        in_specs=[seq, seq, seq,
                  pl.BlockSpec((3, None, ATT_BLOCK, ATT_BLOCK), lambda i, h: (0, h, 0, 0)),
                  vec(DH_DIFF), vec(DH_DIFF), vec(DH_DIFF), vec(DH_DIFF), vec(hd)],
        out_specs=seq,
        out_shape=jax.ShapeDtypeStruct((b, l, H_DIFF * hd), BF16),
        compiler_params=_params("parallel", "parallel"),
        name="diff_attn_prompt",
    )(q16, k16, v16, bias_tab, lq1.reshape(1, -1), lk1.reshape(1, -1), lq2.reshape(1, -1),
      lk2.reshape(1, -1), subln.reshape(1, -1))


def _diff_dec_kernel(pt_ref, qbd_ref, kc_ref, vc_ref, kn_ref, vn_ref, tab_ref,
                     lq1_ref, lk1_ref, lq2_ref, lk2_ref, subln_ref, o_ref,
                     m_ref, l_ref, acc_ref, *, past, n_pages, n_steps, lam_init):
    s_id = pl.program_id(1)
    page = kc_ref.shape[0]
    rows = qbd_ref.shape[0]

    @pl.when(s_id == 0)
    def _():
        m_ref[...] = jnp.full(m_ref.shape, NEG_INF, F32)
        l_ref[...] = jnp.zeros(l_ref.shape, F32)
        acc_ref[...] = jnp.zeros(acc_ref.shape, F32)

    def process(kblk, vblk, kpos0, near):
        kb = kblk.astype(BF16)
        vb = vblk.astype(BF16)
        sc = lax.dot_general(qbd_ref[...], kb, _NT, preferred_element_type=F32)
        shape = (rows, page)
        step = (lax.broadcasted_iota(jnp.int32, shape, 0) >> 3) & (n_steps - 1)
        rel = (past + step) - (kpos0 + lax.broadcasted_iota(jnp.int32, shape, 1))
        if near:
            bucket = _t5_bucket(rel)
            bias = jnp.zeros(shape, F32)
            for b in range(N_BUCKETS):
                bias = jnp.where(bucket == b, tab_ref[:, b:b + 1], bias)
            s = jnp.where(rel >= 0, sc + bias, NEG_INF)
        else:
            s = sc + tab_ref[:, N_BUCKETS - 1:N_BUCKETS]
        m = m_ref[...]
        mn = jnp.maximum(m, jnp.max(s, axis=-1, keepdims=True))
        alpha = jnp.exp(m - mn)
        p = jnp.exp(s - mn)
        l_ref[...] = alpha * l_ref[...] + jnp.sum(p, axis=-1, keepdims=True)
        acc_ref[...] = alpha * acc_ref[...] + jnp.dot(p.astype(BF16), vb,
                                                      preferred_element_type=F32)
        m_ref[...] = mn

    far = (s_id + 1) * page + MAX_DISTANCE <= past

    @pl.when(jnp.logical_and(s_id < n_pages, far))
    def _():
        process(kc_ref[...], vc_ref[...], s_id * page, near=False)

    @pl.when(jnp.logical_and(s_id < n_pages, jnp.logical_not(far)))
    def _():
        process(kc_ref[...], vc_ref[...], s_id * page, near=True)

    @pl.when(s_id == n_pages)
    def _():
        process(kn_ref[...], vn_ref[...], past, near=True)
        lam = _diff_lambda(lq1_ref[...], lk1_ref[...], lq2_ref[...], lk2_ref[...], lam_init)
        subln = subln_ref[...] * (1.0 - lam_init)
        hd = 2 * DH_DIFF
        accn = acc_ref[...] / l_ref[...]
        shape = (H_DIFF, H_DIFF * hd)
        own = lax.broadcasted_iota(jnp.int32, shape, 1) // hd \
            == lax.broadcasted_iota(jnp.int32, shape, 0)
        half = n_steps * H_DIFF
        for t in range(n_steps):
            o1 = jnp.sum(jnp.where(own, accn[t * H_DIFF:(t + 1) * H_DIFF], 0.0),
                         axis=0, keepdims=True)
            o2 = jnp.sum(jnp.where(own, accn[half + t * H_DIFF:half + (t + 1) * H_DIFF], 0.0),
                         axis=0, keepdims=True)
            o = o1 - lam * o2
            for h in range(H_DIFF):
                seg = o[:, h * hd:(h + 1) * hd]
                o_ref[t:t + 1, h * hd:(h + 1) * hd] = _rms(seg, subln)


def _diff_attn_decode(qs16, ks32, vs32, cache_k, cache_v, page_table, rel_bias,
                      lq1, lk1, lq2, lk2, subln, lam_init):
    b, t, dm = qs16.shape
    n_phys, page = cache_k.shape[:2]
    n_pages = page_table.shape[1]
    past = n_pages * page
    hd = 2 * DH_DIFF
    assert t & (t - 1) == 0 and t * H_DIFF * 2 == 64
    kc = cache_k.reshape(n_phys, page, dm)
    vc = cache_v.reshape(n_phys, page, dm)
    col = jnp.arange(dm)
    col_head, col_c = col // hd, (col % hd) // DH_DIFF
    r = jnp.arange(2 * t * H_DIFF)
    r_c, r_t, r_h = r // (t * H_DIFF), (r // H_DIFF) % t, r % H_DIFF
    sel = (col_head[None, :] == r_h[:, None]) & (col_c[None, :] == r_c[:, None])
    qbd = jnp.where(sel[None], (qs16 * (DH_DIFF ** -0.5))[:, r_t, :], 0).astype(BF16)
    pad = ((0, 0), (0, page - t), (0, 0))
    kn = jnp.pad(ks32, pad)
    vn = jnp.pad(vs32, pad)
    tab = jnp.tile(rel_bias.T, (2 * t, 1))
    rows = 2 * t * H_DIFF

    def cache_map(i, s, pt):
        return (pt[i, jnp.minimum(s, n_pages - 1)], 0, 0)

    per_b = lambda i, s, pt: (i, 0, 0)
    vec = lambda n: pl.BlockSpec((1, n), lambda i, s, pt: (0, 0))
    grid_spec = pltpu.PrefetchScalarGridSpec(
        num_scalar_prefetch=1,
        grid=(b, n_pages + 1),
        in_specs=[pl.BlockSpec((None, rows, dm), per_b),
                  pl.BlockSpec((None, page, dm), cache_map),
                  pl.BlockSpec((None, page, dm), cache_map),
                  pl.BlockSpec((None, page, dm), per_b),
                  pl.BlockSpec((None, page, dm), per_b),
                  pl.BlockSpec((rows, N_BUCKETS), lambda i, s, pt: (0, 0)),
                  vec(DH_DIFF), vec(DH_DIFF), vec(DH_DIFF), vec(DH_DIFF), vec(hd)],
        out_specs=pl.BlockSpec((None, t, dm), per_b),
        scratch_shapes=[pltpu.VMEM((rows, 1), F32), pltpu.VMEM((rows, 1), F32),
                        pltpu.VMEM((rows, dm), F32)],
    )
    return pl.pallas_call(
        functools.partial(_diff_dec_kernel, past=past, n_pages=n_pages, n_steps=t,
                          lam_init=lam_init),
        grid_spec=grid_spec,
        out_shape=jax.ShapeDtypeStruct((b, t, dm), F32),
        compiler_params=_params("parallel", "arbitrary"),
        name="diff_attn_decode",
    )(page_table, qbd, kc, vc, kn, vn, tab, lq1.reshape(1, -1), lk1.reshape(1, -1),
      lq2.reshape(1, -1), lk2.reshape(1, -1), subln.reshape(1, -1))


def _suffix_matrix(n):
    r = lax.broadcasted_iota(jnp.int32, (n, 2 * n), 0)
    c = lax.broadcasted_iota(jnp.int32, (n, 2 * n), 1)
    return jnp.where(jnp.logical_or(c >= n, r > c), 1.0, 0.0).astype(BF16)


def _sb_weights(z, carry, suffix, mask):
    n = z.shape[1]
    e = jnp.exp(-jnp.abs(z))
    ls = jnp.minimum(z, 0.0) - jnp.log(1.0 + e)
    lk = ls - z
    if mask is not None:
        lk = jnp.where(mask, lk, 0.0)
    hi = lk.astype(BF16)
    lo = (lk - hi.astype(F32)).astype(BF16)
    t = jnp.dot(jnp.concatenate([hi, lo], axis=0), suffix, preferred_element_type=F32)
    t = t[:z.shape[0]] + t[z.shape[0]:]
    a = jnp.exp(ls + t[:, :n] + carry)
    if mask is not None:
        a = jnp.where(mask, a, 0.0)
    return a, carry + t[:, n:]


def _sb_attn_kernel(q_ref, k_ref, v_ref, o_ref):
    blk = ATT_BLOCK
    nq = q_ref.shape[0] // blk
    lane = lax.broadcasted_iota(jnp.int32, (blk, blk), 1)
    first = lane < DH_SB
    strict = lane < lax.broadcasted_iota(jnp.int32, (blk, blk), 0)
    suffix = _suffix_matrix(blk)
    scale = DH_SB ** -0.5

    def q_block(qi, _):
        q = q_ref[pl.ds(pl.multiple_of(qi * blk, blk), blk), :] * scale
        zero = jnp.zeros_like(q)
        qa = jnp.where(first, q, zero)
        qb = jnp.where(first, zero, q)

        def kv_block(kj, carry, mask):
            ca, acc_a, cb, acc_b = carry
            start = pl.multiple_of(kj * blk, blk)
            ks = k_ref[pl.ds(start, blk), :]
            vs = v_ref[pl.ds(start, blk), :]
            za = lax.dot_general(qa, ks, _NT, preferred_element_type=F32)
            wa, ca = _sb_weights(za, ca, suffix, mask)
            acc_a = acc_a + jnp.dot(wa.astype(BF16), vs, preferred_element_type=F32)
            zb = lax.dot_general(qb, ks, _NT, preferred_element_type=F32)
            wb, cb = _sb_weights(zb, cb, suffix, mask)
            acc_b = acc_b + jnp.dot(wb.astype(BF16), vs, preferred_element_type=F32)
            return ca, acc_a, cb, acc_b

        zeros = jnp.zeros((blk, blk), F32)
        carry = kv_block(qi, (zeros, zeros, zeros, zeros), strict)
        carry = lax.fori_loop(0, qi, lambda i, c: kv_block(qi - 1 - i, c, None), carry)
        _, acc_a, _, acc_b = carry
        o = jnp.where(first, acc_a, acc_b)
        o_ref[pl.ds(pl.multiple_of(qi * blk, blk), blk), :] = o.astype(o_ref.dtype)
        return 0

    lax.fori_loop(0, nq, q_block, 0)


def _sb_attn_prompt(q16, k16, v16):
    b, l, dm = q16.shape
    seq = pl.BlockSpec((None, l, 2 * DH_SB), lambda i, h: (i, 0, h))
    return pl.pallas_call(
        _sb_attn_kernel,
        grid=(b, H_SB // 2),
        in_specs=[seq, seq, seq],
        out_specs=seq,
        out_shape=jax.ShapeDtypeStruct((b, l, dm), BF16),
        compiler_params=_params("parallel", "parallel"),
        name="sb_attn_prompt",
    )(q16, k16, v16)


def _sb_dec_kernel(pt_ref, qbd_ref, kc_ref, vc_ref, kn_ref, vn_ref, o_ref,
                   carry_ref, acc_ref, *, past, n_pages, n_steps):
    s_id = pl.program_id(1)
    page = kc_ref.shape[0]
    rows = qbd_ref.shape[0]
    suffix = _suffix_matrix(page)

    def process(kblk, vblk, kpos0, masked):
        kb = kblk.astype(BF16)
        vb = vblk.astype(BF16)
        z = lax.dot_general(qbd_ref[...], kb, _NT, preferred_element_type=F32)
        mask = None
        if masked:
            shape = (rows, page)
            qpos = past + (lax.broadcasted_iota(jnp.int32, shape, 0) >> 4)
            mask = kpos0 + lax.broadcasted_iota(jnp.int32, shape, 1) < qpos
        w, carry = _sb_weights(z, carry_ref[...], suffix, mask)
        carry_ref[...] = carry
        acc_ref[...] += jnp.dot(w.astype(BF16), vb, preferred_element_type=F32)

    @pl.when(s_id == 0)
    def _():
        carry_ref[...] = jnp.zeros(carry_ref.shape, F32)
        acc_ref[...] = jnp.zeros(acc_ref.shape, F32)
        process(kn_ref[...], vn_ref[...], past, masked=True)

    @pl.when(s_id > 0)
    def _():
        process(kc_ref[...], vc_ref[...], (n_pages - s_id) * page, masked=False)

    @pl.when(s_id == n_pages)
    def _():
        acc = acc_ref[...]
        shape = (H_SB, H_SB * DH_SB)
        own = lax.broadcasted_iota(jnp.int32, shape, 1) // DH_SB \
            == lax.broadcasted_iota(jnp.int32, shape, 0)
        for t in range(n_steps):
            o_ref[t:t + 1, :] = jnp.sum(jnp.where(own, acc[t * H_SB:(t + 1) * H_SB], 0.0),
                                        axis=0, keepdims=True)


def _sb_attn_decode(qs16, ks32, vs32, cache_k, cache_v, page_table):
    b, t, dm = qs16.shape
    n_phys, page = cache_k.shape[:2]
    n_pages = page_table.shape[1]
    past = n_pages * page
    assert t * H_SB == 64
    kc = cache_k.reshape(n_phys, page, dm)
    vc = cache_v.reshape(n_phys, page, dm)
    col_head = jnp.arange(dm) // DH_SB
    r = jnp.arange(t * H_SB)
    r_t, r_h = r // H_SB, r % H_SB
    sel = col_head[None, :] == r_h[:, None]
    qbd = jnp.where(sel[None], (qs16 * (DH_SB ** -0.5))[:, r_t, :], 0).astype(BF16)
    pad = ((0, 0), (0, page - t), (0, 0))
    kn = jnp.pad(ks32, pad)
    vn = jnp.pad(vs32, pad)
    rows = t * H_SB

    def cache_map(i, s, pt):
        return (pt[i, n_pages - jnp.maximum(s, 1)], 0, 0)

    per_b = lambda i, s, pt: (i, 0, 0)
    grid_spec = pltpu.PrefetchScalarGridSpec(
        num_scalar_prefetch=1,
        grid=(b, n_pages + 1),
        in_specs=[pl.BlockSpec((None, rows, dm), per_b),
                  pl.BlockSpec((None, page, dm), cache_map),
                  pl.BlockSpec((None, page, dm), cache_map),
                  pl.BlockSpec((None, page, dm), per_b),
                  pl.BlockSpec((None, page, dm), per_b)],
        out_specs=pl.BlockSpec((None, t, dm), per_b),
        scratch_shapes=[pltpu.VMEM((rows, page), F32), pltpu.VMEM((rows, dm), F32)],
    )
    return pl.pallas_call(
        functools.partial(_sb_dec_kernel, past=past, n_pages=n_pages, n_steps=t),
        grid_spec=grid_spec,
        out_shape=jax.ShapeDtypeStruct((b, t, dm), F32),
        compiler_params=_params("parallel", "arbitrary"),
        name="sb_attn_decode",
    )(page_table, qbd, kc, vc, kn, vn)


def _load_hist(buf_ref, hist_ref, nrows):
    if hist_ref is None:
        buf_ref[0:nrows, :] = jnp.zeros((nrows, buf_ref.shape[1]), F32)
    else:
        buf_ref[0:nrows, :] = hist_ref[...]


def _pool_kernel(*refs, stride, steps, pos0, has_hist):
    if has_hist:
        x_ref, hist_ref, g_ref, w_ref, sc_ref, o_ref, st_ref, buf_ref = refs
    else:
        x_ref, g_ref, w_ref, sc_ref, o_ref, st_ref, buf_ref = refs
        hist_ref = None
    j = pl.program_id(1)
    rows = steps * stride
    hpad = _hist_pad(POOL_HIST, stride)
    hrows = hpad * stride

    @pl.when(j == 0)
    def _():
        _load_hist(buf_ref, hist_ref, hrows)

    x = x_ref[...]
    u = _rms(x, g_ref[...])
    buf_ref[hrows:hrows + rows, :] = u
    step = lax.broadcasted_iota(jnp.int32, (rows, 1), 0) // stride
    pos = pos0 + j * steps + step
    for g, w in enumerate(POOL_WINDOWS):
        c0, c1 = g * POOL_GROUP, (g + 1) * POOL_GROUP
        ug = u[:, c0:c1]
        win = ug
        for i in range(1, w):
            win = win + buf_ref[(hpad - i) * stride:(hpad - i) * stride + rows, c0:c1]
        cnt = jnp.minimum(pos + 1, w).astype(F32)
        d = win / cnt - ug
        y = jnp.dot(d.astype(BF16), w_ref[g], preferred_element_type=F32) * sc_ref[:, c0:c1]
        o_ref[:, c0:c1] = x[:, c0:c1] + y
    last = buf_ref[rows:rows + hrows, :]
    buf_ref[0:hrows, :] = last
    st_ref[...] = last


def _pool_mixer(x3d, hist, g, w16, scale, *, stride, steps, pos0):
    gdim, total, d = x3d.shape
    rows = steps * stride
    nt = total // rows
    hrows = _hist_pad(POOL_HIST, stride) * stride
    tile = pl.BlockSpec((None, rows, d), lambda i, j: (i, j, 0))
    per_g = pl.BlockSpec((None, hrows, d), lambda i, j: (i, 0, 0))
    const2 = lambda i, j: (0, 0)
    in_specs = [tile] + ([per_g] if hist is not None else []) + [
        pl.BlockSpec((1, d), const2),
        pl.BlockSpec(w16.shape, lambda i, j: (0, 0, 0)),
        pl.BlockSpec((1, d), const2)]
    args = [x3d] + ([hist] if hist is not None else []) + [g.reshape(1, d), w16,
                                                          scale.reshape(1, d)]
    return pl.pallas_call(
        functools.partial(_pool_kernel, stride=stride, steps=steps, pos0=pos0,
                          has_hist=hist is not None),
        grid=(gdim, nt),
        in_specs=in_specs,
        out_specs=[tile, per_g],
        out_shape=[jax.ShapeDtypeStruct(x3d.shape, F32),
                   jax.ShapeDtypeStruct((gdim, hrows, d), F32)],
        scratch_shapes=[pltpu.VMEM((hrows + rows, d), F32)],
        compiler_params=_params("parallel", "arbitrary"),
        name="pool_mixer",
    )(*args)


def _conformer_kernel(*refs, stride, steps, has_hist):
    if has_hist:
        (x_ref, hist_ref, g_ref, w1_ref, b1_ref, wdw_ref, bdw_ref, lng_ref, lnb_ref,
         w2_ref, b2_ref, o_ref, st_ref, buf_ref, conv_ref) = refs
    else:
        (x_ref, g_ref, w1_ref, b1_ref, wdw_ref, bdw_ref, lng_ref, lnb_ref,
         w2_ref, b2_ref, o_ref, st_ref, buf_ref, conv_ref) = refs
        hist_ref = None
    j = pl.program_id(1)
    rows = steps * stride
    hpad = _hist_pad(CONV_WIDTH - 1, stride)
    hrows = hpad * stride
    d = x_ref.shape[-1]

    @pl.when(j == 0)
    def _():
        _load_hist(buf_ref, hist_ref, hrows)

    x = x_ref[...]
    u = _rms(x, g_ref[...]).astype(BF16)
    ag = jnp.dot(u, w1_ref[...], preferred_element_type=F32) + b1_ref[...]
    glu = ag[:, :d] * (1.0 / (1.0 + jnp.exp(-ag[:, d:])))
    buf_ref[hrows:hrows + rows, :] = glu
    for c0 in range(0, d, LANES):
        acc = jnp.zeros((rows, LANES), F32) + bdw_ref[:, c0:c0 + LANES]
        for k in range(CONV_WIDTH):
            off = (hpad - (CONV_WIDTH - 1) + k) * stride
            acc = acc + wdw_ref[k:k + 1, c0:c0 + LANES] * buf_ref[off:off + rows, c0:c0 + LANES]
        conv_ref[:, c0:c0 + LANES] = acc
    c = conv_ref[...]
    mu = jnp.mean(c, axis=-1, keepdims=True)
    cc = c - mu
    var = jnp.mean(cc * cc, axis=-1, keepdims=True)
    c = cc * lax.rsqrt(var + EPS) * lng_ref[...] + lnb_ref[...]
    c = c * (1.0 / (1.0 + jnp.exp(-c)))
    y = jnp.dot(c.astype(BF16), w2_ref[...], preferred_element_type=F32) + b2_ref[...]
    o_ref[...] = x + y
    last = buf_ref[rows:rows + hrows, :]
    buf_ref[0:hrows, :] = last
    st_ref[...] = last


def _conformer(x3d, hist, g, w1_16, b1, wdw, bdw, lng, lnb, w2_16, b2, *, stride, steps):
    gdim, total, d = x3d.shape
    rows = steps * stride
    nt = total // rows
    hrows = _hist_pad(CONV_WIDTH - 1, stride) * stride
    tile = pl.BlockSpec((None, rows, d), lambda i, j: (i, j, 0))
    per_g = pl.BlockSpec((None, hrows, d), lambda i, j: (i, 0, 0))
    const2 = lambda i, j: (0, 0)
    full = lambda a: pl.BlockSpec(a.shape, const2)
    wdw_p = jnp.pad(wdw, ((0, -CONV_WIDTH % SUBLANES), (0, 0)))
    params = [g.reshape(1, d), w1_16, b1.reshape(1, -1), wdw_p, bdw.reshape(1, d),
              lng.reshape(1, d), lnb.reshape(1, d), w2_16, b2.reshape(1, d)]
    in_specs = [tile] + ([per_g] if hist is not None else []) + [full(p) for p in params]
    args = [x3d] + ([hist] if hist is not None else []) + params
    return pl.pallas_call(
        functools.partial(_conformer_kernel, stride=stride, steps=steps,
                          has_hist=hist is not None),
        grid=(gdim, nt),
        in_specs=in_specs,
        out_specs=[tile, per_g],
        out_shape=[jax.ShapeDtypeStruct(x3d.shape, F32),
                   jax.ShapeDtypeStruct((gdim, hrows, d), F32)],
        scratch_shapes=[pltpu.VMEM((hrows + rows, d), F32), pltpu.VMEM((rows, d), F32)],
        compiler_params=_params("parallel", "arbitrary"),
        name="conformer_conv",
    )(*args)


def _ffn_kernel(*refs, stride, steps, has_hist, final_norm):
    refs = list(refs)
    x_ref = refs.pop(0)
    hist_ref = refs.pop(0) if has_hist else None
    g_ref, wup_ref, wdw_ref, wdn_ref = refs[:4]
    refs = refs[4:]
    gf_ref = refs.pop(0) if final_norm else None
    o_ref, st_ref, carry_ref, buf_ref, acc_ref = refs
    j = pl.program_id(1)
    rows = steps * stride
    hrows = _hist_pad(FFN_CONV_WIDTH - 1, stride) * stride
    n_chunks, _, two_tc = wup_ref.shape
    tc = two_tc // 2

    @pl.when(j == 0)
    def _():
        if hist_ref is None:
            carry_ref[...] = jnp.zeros(carry_ref.shape, F32)
        else:
            carry_ref[...] = hist_ref[...]

    x = x_ref[...]
    h = _rms(x, g_ref[...]).astype(BF16)
    for c in range(n_chunks):
        up = jnp.dot(h, wup_ref[c], preferred_element_type=F32)
        buf_ref[0:hrows, :] = carry_ref[c]
        buf_ref[hrows:hrows + rows, :] = up
        w = wdw_ref[c]
        back1 = buf_ref[hrows - stride:hrows - stride + rows, :]
        back2 = buf_ref[hrows - 2 * stride:hrows - 2 * stride + rows, :]
        conv = w[0:1] * back2 + w[1:2] * back1 + w[2:3] * up
        carry_ref[c] = buf_ref[rows:rows + hrows, :]
        gate = conv[:, :tc]
        act = gate * (1.0 / (1.0 + jnp.exp(-gate))) * conv[:, tc:]
        part = jnp.dot(act.astype(BF16), wdn_ref[c], preferred_element_type=F32)
        if c == 0:
            acc_ref[...] = part
        else:
            acc_ref[...] += part
    y = x + acc_ref[...]
    if final_norm:
        y = _rms(y, gf_ref[...])
    o_ref[...] = y

    @pl.when(j == pl.num_programs(1) - 1)
    def _():
        st_ref[...] = carry_ref[...]


def _ffn(x3d, hist, g, wup_c, wdw_c, wdn_c, g_final, *, stride, steps):
    gdim, total, d = x3d.shape
    rows = steps * stride
    nt = total // rows
    hrows = _hist_pad(FFN_CONV_WIDTH - 1, stride) * stride
    n_chunks, _, two_tc = wup_c.shape
    tile = pl.BlockSpec((None, rows, d), lambda i, j: (i, j, 0))
    st_blk = pl.BlockSpec((None, n_chunks, hrows, two_tc), lambda i, j: (i, 0, 0, 0))
    const2 = lambda i, j: (0, 0)
    const3 = lambda i, j: (0, 0, 0)
    in_specs = [tile] + ([st_blk] if hist is not None else []) + [
        pl.BlockSpec((1, d), const2), pl.BlockSpec(wup_c.shape, const3),
        pl.BlockSpec(wdw_c.shape, const3), pl.BlockSpec(wdn_c.shape, const3)]
    args = [x3d] + ([hist] if hist is not None else []) + [g.reshape(1, d), wup_c, wdw_c, wdn_c]
    if g_final is not None:
        in_specs.append(pl.BlockSpec((1, d), const2))
        args.append(g_final.reshape(1, d))
    return pl.pallas_call(
        functools.partial(_ffn_kernel, stride=stride, steps=steps, has_hist=hist is not None,
                          final_norm=g_final is not None),
        grid=(gdim, nt),
        in_specs=in_specs,
        out_specs=[tile, st_blk],
        out_shape=[jax.ShapeDtypeStruct(x3d.shape, F32),
                   jax.ShapeDtypeStruct((gdim, n_chunks, hrows, two_tc), F32)],
        scratch_shapes=[pltpu.VMEM((n_chunks, hrows, two_tc), F32),
                        pltpu.VMEM((hrows + rows, two_tc), F32),
                        pltpu.VMEM((rows, d), F32)],
        compiler_params=_params("parallel", "arbitrary"),
        name="conv_ffn",
    )(*args)


def _ffn_weights(w_up, w_dw, w_down):
    d, two_f = w_up.shape
    f = two_f // 2
    tc = FFN_CHUNK
    nc = f // tc
    assert nc * tc == f
    wup_c = w_up.reshape(d, 2, nc, tc).transpose(2, 0, 1, 3).reshape(nc, d, 2 * tc).astype(BF16)
    wdw_c = w_dw.reshape(FFN_CONV_WIDTH, 2, nc, tc).transpose(2, 0, 1, 3).reshape(
        nc, FFN_CONV_WIDTH, 2 * tc)
    wdn_c = w_down.reshape(nc, tc, w_down.shape[1]).astype(BF16)
    return wup_c, wdw_c, wdn_c


def _ffn_state_from_chunks(st, stride):
    gdim, nc, _, two_tc = st.shape
    keep = (FFN_CONV_WIDTH - 1) * stride
    st = st[:, :, -keep:, :].reshape(gdim, nc, keep, 2, two_tc // 2)
    return st.transpose(0, 2, 3, 1, 4).reshape(gdim, keep, nc * two_tc)


def _ffn_hist_to_chunks(hist_rows, nc):
    r, two_f = hist_rows.shape
    stride = r // (FFN_CONV_WIDTH - 1)
    tc = two_f // 2 // nc
    hrows = _hist_pad(FFN_CONV_WIDTH - 1, stride) * stride
    h = jnp.pad(hist_rows, ((hrows - r, 0), (0, 0)))
    h = h.reshape(hrows, 2, nc, tc).transpose(2, 0, 1, 3)
    return h.reshape(1, nc, hrows, 2 * tc)


def _pad_hist(hist_rows, need, stride):
    extra = (_hist_pad(need, stride) - need) * stride
    return jnp.pad(hist_rows, ((extra, 0), (0, 0)))


def _time_major(a):
    a = jnp.swapaxes(a, 0, 1)
    return a.reshape(a.shape[0] * a.shape[1], *a.shape[2:])


def _batch_major(a, nb):
    a = a.reshape(a.shape[0] // nb, nb, *a.shape[1:])
    return jnp.swapaxes(a, 0, 1)


def kernel(x_prompt, x_sample, cache_k_diff, cache_v_diff, cache_k_sb, cache_v_sb, state_pool, state_conv, state_ffn, page_table, rel_bias, norm_mix, norm_ffn, norm_final, diff_w_qkv, diff_w_o, diff_lambda_q1, diff_lambda_k1, diff_lambda_q2, diff_lambda_k2, diff_subln, sb_w_qkv, sb_w_o, pool_w, pool_scale, conv_w_pw1, conv_b_pw1, conv_w_dw, conv_b_dw, conv_ln_g, conv_ln_b, conv_w_pw2, conv_b_pw2, ffn_w_up, ffn_w_dw, ffn_w_down):
    bp, lp, d = x_prompt.shape
    bs, ls, _ = x_sample.shape
    depth = ffn_w_up.shape[0]
    past = page_table.shape[1] * cache_k_diff.shape[1]
    seq_tile = min(lp, 256)

    xp = x_prompt
    xs = _time_major(x_sample)[None]
    ffn_p, ffn_s = [], []
    outs = {}
    for layer in range(depth):
        kind = layer % 4
        if kind == 0:
            lam_init = 0.8 - 0.6 * math.exp(-0.3 * layer)
            w16 = diff_w_qkv.astype(BF16)
            wo16 = diff_w_o.astype(BF16)
            lams = (diff_lambda_q1, diff_lambda_k1, diff_lambda_q2, diff_lambda_k2)
            q16, k32, v32, k16, v16 = _qkv_proj(xp.reshape(bp * lp, d), norm_mix[layer], w16)
            shp = (bp, lp, d)
            attn = _diff_attn_prompt(q16.reshape(shp), k16.reshape(shp), v16.reshape(shp),
                                     _bias_table(rel_bias), *lams, diff_subln, lam_init)
            xp = _mm_resid(attn.reshape(bp * lp, d), wo16, xp.reshape(bp * lp, d)).reshape(shp)
            outs['k_diff_p'] = k32.reshape(bp, lp, H_DIFF, 2 * DH_DIFF)
            outs['v_diff_p'] = v32.reshape(bp, lp, H_DIFF, 2 * DH_DIFF)

            q16, k32, v32, _, _ = _qkv_proj(xs[0], norm_mix[layer], w16)
            qs, ks, vs = (_batch_major(a, bs) for a in (q16, k32, v32))
            attn = _diff_attn_decode(qs, ks, vs, cache_k_diff, cache_v_diff, page_table,
                                     rel_bias, *lams, diff_subln, lam_init)
            xs = _mm_resid(_time_major(attn), wo16, xs[0])[None]
            outs['k_diff_s'] = ks.reshape(bs, ls, H_DIFF, 2 * DH_DIFF)
            outs['v_diff_s'] = vs.reshape(bs, ls, H_DIFF, 2 * DH_DIFF)
        elif kind == 1:
            w16 = sb_w_qkv.astype(BF16)
            wo16 = sb_w_o.astype(BF16)
            q16, k32, v32, k16, v16 = _qkv_proj(xp.reshape(bp * lp, d), norm_mix[layer], w16)
            shp = (bp, lp, d)
            attn = _sb_attn_prompt(q16.reshape(shp), k16.reshape(shp), v16.reshape(shp))
            xp = _mm_resid(attn.reshape(bp * lp, d), wo16, xp.reshape(bp * lp, d)).reshape(shp)
            outs['k_sb_p'] = k32.reshape(bp, lp, H_SB, DH_SB)
            outs['v_sb_p'] = v32.reshape(bp, lp, H_SB, DH_SB)

            q16, k32, v32, _, _ = _qkv_proj(xs[0], norm_mix[layer], w16)
            qs, ks, vs = (_batch_major(a, bs) for a in (q16, k32, v32))
            attn = _sb_attn_decode(qs, ks, vs, cache_k_sb, cache_v_sb, page_table)
            xs = _mm_resid(_time_major(attn), wo16, xs[0])[None]
            outs['k_sb_s'] = ks.reshape(bs, ls, H_SB, DH_SB)
            outs['v_sb_s'] = vs.reshape(bs, ls, H_SB, DH_SB)
        elif kind == 2:
            pw16 = pool_w.astype(BF16)
            xp, st = _pool_mixer(xp, None, norm_mix[layer], pw16, pool_scale,
                                 stride=1, steps=seq_tile, pos0=0)
            outs['pool_p'] = st[:, -POOL_HIST:]
            hist = _pad_hist(_time_major(state_pool), POOL_HIST, bs)[None]
            xs, st = _pool_mixer(xs, hist, norm_mix[layer], pw16, pool_scale,
                                 stride=bs, steps=ls, pos0=past)
            outs['pool_s'] = _batch_major(st[0, -bs * POOL_HIST:], bs)
        else:
            w1 = conv_w_pw1.astype(BF16)
            w2 = conv_w_pw2.astype(BF16)
            cargs = (norm_mix[layer], w1, conv_b_pw1, conv_w_dw, conv_b_dw, conv_ln_g,
                     conv_ln_b, w2, conv_b_pw2)
            keep = CONV_WIDTH - 1
            xp, st = _conformer(xp, None, *cargs, stride=1, steps=seq_tile)
            outs['conv_p'] = st[:, -keep:]
            hist = _pad_hist(_time_major(state_conv), keep, bs)[None]
            xs, st = _conformer(xs, hist, *cargs, stride=bs, steps=ls)
            outs['conv_s'] = _batch_major(st[0, -bs * keep:], bs)

        wup_c, wdw_c, wdn_c = _ffn_weights(ffn_w_up[layer], ffn_w_dw[layer], ffn_w_down[layer])
        g_final = norm_final if layer == depth - 1 else None
        xp, st = _ffn(xp, None, norm_ffn[layer], wup_c, wdw_c, wdn_c, g_final,
                      stride=1, steps=seq_tile)
        ffn_p.append(_ffn_state_from_chunks(st, 1))
        hist = _ffn_hist_to_chunks(_time_major(state_ffn[layer]), wup_c.shape[0])
        xs, st = _ffn(xs, hist, norm_ffn[layer], wup_c, wdw_c, wdn_c, g_final,
                      stride=bs, steps=ls)
        ffn_s.append(_batch_major(_ffn_state_from_chunks(st, bs)[0], bs))

    y_prompt = xp
    y_sample = _batch_major(xs[0], bs)
    return (y_prompt, y_sample,
            outs['k_diff_p'], outs['v_diff_p'], outs['k_sb_p'], outs['v_sb_p'],
            outs['pool_p'], outs['conv_p'], jnp.stack(ffn_p, axis=0),
            outs['k_diff_s'], outs['v_diff_s'], outs['k_sb_s'], outs['v_sb_s'],
            outs['pool_s'], outs['conv_s'], jnp.stack(ffn_s, axis=0))
```

```python
import functools
import math

import jax
import jax.numpy as jnp
from jax import lax
from jax.experimental import pallas as pl
from jax.experimental.pallas import tpu as pltpu

D_MODEL = 1024
H_DIFF = 8
DH_DIFF = 64
H_SB = 16
DH_SB = 64
N_BUCKETS = 32
MAX_DISTANCE = 128
POOL_WINDOWS = (2, 4, 8, 16)
POOL_GROUP = D_MODEL // len(POOL_WINDOWS)
POOL_HIST = max(POOL_WINDOWS) - 1
CONV_WIDTH = 31
FFN_CONV_WIDTH = 3
EPS = 1e-6
NEG_INF = -1e30

BF16 = jnp.bfloat16
F32 = jnp.float32

V7X_VMEM_BYTES = 64 * 1024 * 1024
VMEM_LIMIT = V7X_VMEM_BYTES - 12 * 1024 * 1024
SUBLANES = 8
LANES = 128

BF16_ROWS = 16
ATT_BLOCK = 256
FFN_CHUNK = 256
DECODE_PAGES_PER_STEP = 4


def _hist_pad(need, stride):
    h = need
    while (h * stride) % SUBLANES:
        h += 1
    return h

_NT = (((1,), (1,)), ((), ()))


def _params(*sem):
    return pltpu.CompilerParams(dimension_semantics=sem, vmem_limit_bytes=VMEM_LIMIT)


def _rms(x, g):
    return x * lax.rsqrt(jnp.mean(x * x, axis=-1, keepdims=True) + EPS) * g


def _row_tile(m, want):
    t = min(m, want)
    assert m % t == 0, (m, t)
    return t


def _qkv_kernel(x_ref, g_ref, w_ref, *refs, with_vt):
    if with_vt:
        wvt_ref, q16_ref, k32_ref, v32_ref, k16_ref, vt16_ref = refs
    else:
        q16_ref, k32_ref, v32_ref = refs
    n = q16_ref.shape[-1]
    u = _rms(x_ref[...], g_ref[...]).astype(BF16)
    q = jnp.dot(u, w_ref[:, 0:n], preferred_element_type=F32)
    q16_ref[...] = q.astype(BF16)
    k = jnp.dot(u, w_ref[:, n:2 * n], preferred_element_type=F32)
    k32_ref[...] = k
    v32_ref[...] = jnp.dot(u, w_ref[:, 2 * n:3 * n], preferred_element_type=F32)
    if with_vt:
        k16_ref[...] = k.astype(BF16)
        vt = lax.dot_general(wvt_ref[...], u, _NT, preferred_element_type=F32)
        vt16_ref[...] = vt.astype(BF16)


def _qkv_proj(x2d, g, w16, wvt16=None):
    m, d = x2d.shape
    n = w16.shape[1] // 3
    tm = _row_tile(m, 256)
    row = lambda i: (i, 0)
    const = lambda i: (0, 0)
    out_blk = pl.BlockSpec((tm, n), row)
    in_specs = [pl.BlockSpec((tm, d), row), pl.BlockSpec((1, d), const),
                pl.BlockSpec((d, 3 * n), const)]
    out_specs = [out_blk] * 3
    out_shape = [jax.ShapeDtypeStruct((m, n), BF16), jax.ShapeDtypeStruct((m, n), F32),
                 jax.ShapeDtypeStruct((m, n), F32)]
    args = [x2d, g.reshape(1, d), w16]
    if wvt16 is not None:
        in_specs.append(pl.BlockSpec((n, d), const))
        out_specs += [out_blk, pl.BlockSpec((n, tm), lambda i: (0, i))]
        out_shape += [jax.ShapeDtypeStruct((m, n), BF16), jax.ShapeDtypeStruct((n, m), BF16)]
        args.append(wvt16)
    return pl.pallas_call(
        functools.partial(_qkv_kernel, with_vt=wvt16 is not None),
        grid=(m // tm,),
        in_specs=in_specs,
        out_specs=out_specs,
        out_shape=out_shape,
        compiler_params=_params("parallel"),
        name="qkv_proj",
    )(*args)


def _mm_resid_kernel(a_ref, w_ref, r_ref, o_ref):
    o_ref[...] = r_ref[...] + jnp.dot(a_ref[...].astype(BF16), w_ref[...],
                                      preferred_element_type=F32)


def _mm_resid(a2d, w16, resid2d):
    m, k = a2d.shape
    n = w16.shape[1]
    tm = _row_tile(m, 512)
    row = lambda i: (i, 0)
    return pl.pallas_call(
        _mm_resid_kernel,
        grid=(m // tm,),
        in_specs=[pl.BlockSpec((tm, k), row), pl.BlockSpec((k, n), lambda i: (0, 0)),
                  pl.BlockSpec((tm, n), row)],
        out_specs=pl.BlockSpec((tm, n), row),
        out_shape=jax.ShapeDtypeStruct((m, n), F32),
        compiler_params=_params("parallel"),
        name="out_proj_resid",
    )(a2d, w16, resid2d)


def _t5_bucket(rel):
    n = jnp.maximum(rel, 0)
    max_exact = N_BUCKETS // 2
    nf = jnp.maximum(n, 1).astype(F32)
    large = max_exact + (jnp.log(nf / max_exact) / math.log(MAX_DISTANCE / max_exact)
                         * (N_BUCKETS - max_exact)).astype(jnp.int32)
    large = jnp.minimum(large, N_BUCKETS - 1)
    return jnp.where(n < max_exact, n, large)


def _bias_table_kernel(rb_ref, o_ref):
    d = pl.program_id(0)
    shape = (ATT_BLOCK, ATT_BLOCK)
    rel = d * ATT_BLOCK + lax.broadcasted_iota(jnp.int32, shape, 1) \
        - lax.broadcasted_iota(jnp.int32, shape, 0)
    bucket = _t5_bucket(rel)
    for h in range(H_DIFF):
        val = jnp.zeros(shape, F32)
        for b in range(N_BUCKETS):
            val = jnp.where(bucket == b, rb_ref[b, h], val)
        o_ref[h] = jnp.where(rel >= 0, val, NEG_INF)


def _bias_table(rel_bias):
    assert ATT_BLOCK >= MAX_DISTANCE
    return pl.pallas_call(
        _bias_table_kernel,
        grid=(3,),
        in_specs=[pl.BlockSpec(memory_space=pltpu.SMEM)],
        out_specs=pl.BlockSpec((None, H_DIFF, ATT_BLOCK, ATT_BLOCK), lambda d: (d, 0, 0, 0)),
        out_shape=jax.ShapeDtypeStruct((3, H_DIFF, ATT_BLOCK, ATT_BLOCK), F32),
        compiler_params=_params("parallel"),
        name="t5_bias_table",
    )(rel_bias)


def _diff_lambda(lq1, lk1, lq2, lk2, lam_init):
    e1 = jnp.exp(jnp.sum(lq1 * lk1, axis=-1, keepdims=True))
    e2 = jnp.exp(jnp.sum(lq2 * lk2, axis=-1, keepdims=True))
    return e1 - e2 + lam_init


def _load_vt_chunks(vt_ref, vx_ref, ones_rows):
    blk = ATT_BLOCK
    hd = vt_ref.shape[0]
    for j in range(vx_ref.shape[0]):
        vx_ref[j, 0:hd, :] = vt_ref[:, j * blk:(j + 1) * blk]
        if ones_rows:
            vx_ref[j, hd:hd + ones_rows, :] = jnp.ones((ones_rows, blk), BF16)


def _diff_attn_kernel(q_ref, k_ref, vt_ref, bias_ref, lq1_ref, lk1_ref, lq2_ref, lk2_ref,
                      subln_ref, o_ref, vx_ref, *, lam_init):
    blk = ATT_BLOCK
    hd = 2 * DH_DIFF
    nq = q_ref.shape[0] // blk
    first = lax.broadcasted_iota(jnp.int32, (blk, hd), 1) < DH_DIFF
    lam = _diff_lambda(lq1_ref[...], lk1_ref[...], lq2_ref[...], lk2_ref[...], lam_init)
    subln = subln_ref[...] * (1.0 - lam_init)
    scale = DH_DIFF ** -0.5
    _load_vt_chunks(vt_ref, vx_ref, BF16_ROWS)

    def q_block(qi, _):
        q = q_ref[pl.ds(pl.multiple_of(qi * blk, blk), blk), :] * scale
        zero = jnp.zeros_like(q)
        qa = jnp.where(first, q, zero)
        qb = jnp.where(first, zero, q)

        def scores(kj):
            ks = k_ref[pl.ds(pl.multiple_of(kj * blk, blk), blk), :]
            bias = bias_ref[jnp.minimum(qi - kj, 2)]
            return (lax.dot_general(ks, qa, _NT, preferred_element_type=F32) + bias,
                    lax.dot_general(ks, qb, _NT, preferred_element_type=F32) + bias)

        def update(kj, s, m, a):
            mn = jnp.maximum(m, jnp.max(s, axis=0, keepdims=True))
            alpha = jnp.exp(m - mn)
            p = jnp.exp(s - mn)
            a = alpha * a + jnp.dot(vx_ref[kj], p.astype(BF16), preferred_element_type=F32)
            return mn, a

        def kv_block(kj, carry):
            s1, s2, m1, a1, m2, a2 = carry
            n1, n2 = scores(kj + 1)
            m1, a1 = update(kj, s1, m1, a1)
            m2, a2 = update(kj, s2, m2, a2)
            return n1, n2, m1, a1, m2, a2

        m0 = jnp.full((1, blk), NEG_INF, F32)
        a0 = jnp.zeros((hd + BF16_ROWS, blk), F32)
        s1, s2, m1, a1, m2, a2 = lax.fori_loop(0, qi, kv_block, scores(0) + (m0, a0, m0, a0))
        m1, a1 = update(qi, s1, m1, a1)
        m2, a2 = update(qi, s2, m2, a2)
        o = a1[:hd] * (1.0 / a1[hd:hd + 1]) - lam * (a2[:hd] * (1.0 / a2[hd:hd + 1]))
        o = o * lax.rsqrt(jnp.mean(o * o, axis=0, keepdims=True) + EPS)
        o_ref[pl.ds(pl.multiple_of(qi * blk, blk), blk), :] = (o.T * subln).astype(o_ref.dtype)
        return 0

    lax.fori_loop(0, nq, q_block, 0)


def _diff_attn_prompt(q16, k16, vt16, bias_tab, lq1, lk1, lq2, lk2, subln, lam_init):
    b, l, _ = q16.shape
    hd = 2 * DH_DIFF
    seq = pl.BlockSpec((None, l, hd), lambda i, h: (i, 0, h))
    vec = lambda n: pl.BlockSpec((1, n), lambda i, h: (0, 0))
    return pl.pallas_call(
        functools.partial(_diff_attn_kernel, lam_init=lam_init),
        grid=(b, H_DIFF),
        in_specs=[seq, seq, pl.BlockSpec((hd, l), lambda i, h: (h, i)),
                  pl.BlockSpec((3, None, ATT_BLOCK, ATT_BLOCK), lambda i, h: (0, h, 0, 0)),
                  vec(DH_DIFF), vec(DH_DIFF), vec(DH_DIFF), vec(DH_DIFF), vec(hd)],
        out_specs=seq,
        out_shape=jax.ShapeDtypeStruct((b, l, H_DIFF * hd), BF16),
        scratch_shapes=[pltpu.VMEM((l // ATT_BLOCK, hd + BF16_ROWS, ATT_BLOCK), BF16)],
        compiler_params=_params("parallel", "parallel"),
        name="diff_attn_prompt",
    )(q16, k16, vt16, bias_tab, lq1.reshape(1, -1), lk1.reshape(1, -1), lq2.reshape(1, -1),
      lk2.reshape(1, -1), subln.reshape(1, -1))


def _diff_dec_kernel(pt_ref, qm_ref, *refs, past, n_groups, group, n_steps, lam_init):
    kc_refs, vc_refs = refs[:group], refs[group:2 * group]
    (kn_ref, vn_ref, tab_ref, lq1_ref, lk1_ref, lq2_ref, lk2_ref, subln_ref, o_ref,
     m_ref, l_ref, acc_ref, far_ref) = refs[2 * group:]
    s_id = pl.program_id(1)
    page = kc_refs[0].shape[0]
    rows = qm_ref.shape[0]
    hd = 2 * DH_DIFF
    last_bucket = tab_ref[:, N_BUCKETS - 1:N_BUCKETS]

    def own_head(shape):
        return (lax.broadcasted_iota(jnp.int32, shape, 1) & (H_DIFF - 1)) \
            == (lax.broadcasted_iota(jnp.int32, shape, 0) & (H_DIFF - 1))

    @pl.when(s_id == 0)
    def _():
        m_ref[...] = jnp.full(m_ref.shape, NEG_INF, F32)
        l_ref[...] = jnp.zeros(l_ref.shape, F32)
        acc_ref[...] = jnp.zeros(acc_ref.shape, F32)
        far_ref[...] = jnp.where(own_head(far_ref.shape), last_bucket, NEG_INF)

    def near_term(keys, kpos0):
        shape = (rows, keys * H_DIFF)
        step = (lax.broadcasted_iota(jnp.int32, shape, 0) >> 3) & (n_steps - 1)
        rel = (past + step) - (kpos0 + (lax.broadcasted_iota(jnp.int32, shape, 1) >> 3))
        bucket = _t5_bucket(rel)
        bias = jnp.zeros(shape, F32)
        for b in range(N_BUCKETS):
            bias = jnp.where(bucket == b, tab_ref[:, b:b + 1], bias)
        return jnp.where(jnp.logical_and(own_head(shape), rel >= 0), bias, NEG_INF)

    def process(blocks):
        scores, values = [], []
        for k_ref, v_ref, term in blocks:
            keys = k_ref.shape[0]
            kb = k_ref[...].reshape(keys * H_DIFF, hd).astype(BF16)
            values.append(v_ref[...].reshape(keys * H_DIFF, hd).astype(BF16))
            scores.append(lax.dot_general(qm_ref[...], kb, _NT, preferred_element_type=F32)
                          + term)
        m = m_ref[...]
        mn = m
        for s in scores:
            mn = jnp.maximum(mn, jnp.max(s, axis=-1, keepdims=True))
        alpha = jnp.exp(m - mn)
        l = alpha * l_ref[...]
        acc = alpha * acc_ref[...]
        for s, vb in zip(scores, values):
            p = jnp.exp(s - mn)
            l = l + jnp.sum(p, axis=-1, keepdims=True)
            acc = acc + jnp.dot(p.astype(BF16), vb, preferred_element_type=F32)
        l_ref[...] = l
        acc_ref[...] = acc
        m_ref[...] = mn

    @pl.when(s_id < n_groups)
    def _():
        blocks = []
        for g in range(group):
            kpos0 = (s_id * group + g) * page
            far = kpos0 + page + MAX_DISTANCE <= past
            term = lax.cond(far, lambda: far_ref[...],
                            functools.partial(near_term, page, kpos0))
            blocks.append((kc_refs[g], vc_refs[g], term))
        process(blocks)

    @pl.when(s_id == n_groups)
    def _():
        process([(kn_ref, vn_ref, near_term(kn_ref.shape[0], past))])
        lam = _diff_lambda(lq1_ref[...], lk1_ref[...], lq2_ref[...], lk2_ref[...], lam_init)
        subln = subln_ref[...] * (1.0 - lam_init)
        accn = acc_ref[...] * (1.0 / l_ref[...])
        half = n_steps * H_DIFF
        o = accn[:half] - lam * accn[half:]
        o_ref[...] = _rms(o, subln).reshape(o_ref.shape)


def _diff_attn_decode(qs16, ks32, vs32, cache_k, cache_v, page_table, rel_bias,
                      lq1, lk1, lq2, lk2, subln, lam_init):
    b, t, dm = qs16.shape
    n_phys, page = cache_k.shape[:2]
    n_pages = page_table.shape[1]
    past = n_pages * page
    hd = 2 * DH_DIFF
    assert t & (t - 1) == 0 and t <= SUBLANES and cache_k.shape[2:] == (H_DIFF, hd)
    half_c = (jnp.arange(hd) // DH_DIFF)[None, :] == jnp.arange(2)[:, None]
    q4 = (qs16 * (DH_DIFF ** -0.5)).reshape(b, 1, t, H_DIFF, hd)
    qm = jnp.where(half_c[None, :, None, None, :], q4, 0).astype(BF16)
    rows = 2 * t * H_DIFF
    qm = qm.reshape(b, rows, hd)
    pad = ((0, 0), (0, SUBLANES - t), (0, 0), (0, 0))
    kn = jnp.pad(ks32.reshape(b, t, H_DIFF, hd), pad)
    vn = jnp.pad(vs32.reshape(b, t, H_DIFF, hd), pad)
    tab = jnp.tile(rel_bias.T, (2 * t, 1))

    group = DECODE_PAGES_PER_STEP
    n_groups = n_pages // group
    assert n_groups * group == n_pages

    def cache_spec(g):
        def index(i, s, pt):
            return (pt[i, jnp.minimum(s, n_groups - 1) * group + g], 0, 0, 0)
        return pl.BlockSpec((None, page, H_DIFF, hd), index)

    per_b3 = lambda i, s, pt: (i, 0, 0)
    per_b4 = lambda i, s, pt: (i, 0, 0, 0)
    vec = lambda n: pl.BlockSpec((1, n), lambda i, s, pt: (0, 0))
    grid_spec = pltpu.PrefetchScalarGridSpec(
        num_scalar_prefetch=1,
        grid=(b, n_groups + 1),
        in_specs=[pl.BlockSpec((None, rows, hd), per_b3)]
                 + [cache_spec(g) for g in range(group)] * 2
                 + [pl.BlockSpec((None, SUBLANES, H_DIFF, hd), per_b4),
                  pl.BlockSpec((None, SUBLANES, H_DIFF, hd), per_b4),
                  pl.BlockSpec((rows, N_BUCKETS), lambda i, s, pt: (0, 0)),
                  vec(DH_DIFF), vec(DH_DIFF), vec(DH_DIFF), vec(DH_DIFF), vec(hd)],
        out_specs=pl.BlockSpec((None, t, H_DIFF, hd), per_b4),
        scratch_shapes=[pltpu.VMEM((rows, 1), F32), pltpu.VMEM((rows, 1), F32),
                        pltpu.VMEM((rows, hd), F32),
                        pltpu.VMEM((rows, page * H_DIFF), F32)],
    )
    return pl.pallas_call(
        functools.partial(_diff_dec_kernel, past=past, n_groups=n_groups, group=group,
                          n_steps=t, lam_init=lam_init),
        grid_spec=grid_spec,
        out_shape=jax.ShapeDtypeStruct((b, t, H_DIFF, hd), F32),
        compiler_params=_params("parallel", "arbitrary"),
        name="diff_attn_decode",
    )(page_table, qm, *([cache_k] * group), *([cache_v] * group), kn, vn, tab,
      lq1.reshape(1, -1), lk1.reshape(1, -1), lq2.reshape(1, -1), lk2.reshape(1, -1),
      subln.reshape(1, -1))


def _suffix_matrix(n):
    r = lax.broadcasted_iota(jnp.int32, (n, 2 * n), 0)
    c = lax.broadcasted_iota(jnp.int32, (n, 2 * n), 1)
    return jnp.where(jnp.logical_or(c >= n, r > c), 1.0, 0.0).astype(BF16)


def _sb_weights(z, carry, suffix, mask):
    n = z.shape[1]
    e = jnp.exp(-jnp.abs(z))
    ls = jnp.minimum(z, 0.0) - jnp.log(1.0 + e)
    lk = ls - z
    if mask is not None:
        lk = jnp.where(mask, lk, 0.0)
    hi = lk.astype(BF16)
    lo = (lk - hi.astype(F32)).astype(BF16)
    t = jnp.dot(jnp.concatenate([hi, lo], axis=0), suffix, preferred_element_type=F32)
    t = t[:z.shape[0]] + t[z.shape[0]:]
    a = jnp.exp(ls + t[:, :n] + carry)
    if mask is not None:
        a = jnp.where(mask, a, 0.0)
    return a, carry + t[:, n:]


def _sb_weights_t(z, carry, suffix_t, mask):
    n = z.shape[0]
    e = jnp.exp(-jnp.abs(z))
    ls = jnp.minimum(z, 0.0) - jnp.log(1.0 + e)
    lk = ls - z
    if mask is not None:
        lk = jnp.where(mask, lk, 0.0)
    t = jnp.dot(suffix_t, lk.astype(BF16), preferred_element_type=F32)
    a = jnp.exp(ls + t[:n] + carry)
    if mask is not None:
        a = jnp.where(mask, a, 0.0)
    return a, carry + t[n:n + 1]


def _sb_attn_kernel(q_ref, k_ref, vt_ref, o_ref, vx_ref):
    blk = ATT_BLOCK
    nq = q_ref.shape[0] // blk
    first = lax.broadcasted_iota(jnp.int32, (blk, 2 * DH_SB), 1) < DH_SB
    rows = lax.broadcasted_iota(jnp.int32, (blk + BF16_ROWS, blk), 0)
    cols = lax.broadcasted_iota(jnp.int32, (blk + BF16_ROWS, blk), 1)
    suffix_t = jnp.where(jnp.logical_or(rows >= blk, cols > rows), 1.0, 0.0).astype(BF16)
    strict = lax.broadcasted_iota(jnp.int32, (blk, blk), 0) \
        < lax.broadcasted_iota(jnp.int32, (blk, blk), 1)
    upper = lax.broadcasted_iota(jnp.int32, (2 * DH_SB, blk), 0) < DH_SB
    scale = DH_SB ** -0.5
    _load_vt_chunks(vt_ref, vx_ref, 0)

    def q_block(qi, _):
        q = q_ref[pl.ds(pl.multiple_of(qi * blk, blk), blk), :] * scale
        zero = jnp.zeros_like(q)
        qa = jnp.where(first, q, zero)
        qb = jnp.where(first, zero, q)

        def scores(kj):
            ks = k_ref[pl.ds(pl.multiple_of(kj * blk, blk), blk), :]
            return (lax.dot_general(ks, qa, _NT, preferred_element_type=F32),
                    lax.dot_general(ks, qb, _NT, preferred_element_type=F32))

        def update(kj, za, zb, state, mask):
            ca, acc_a, cb, acc_b = state
            vx = vx_ref[kj]
            wa, ca = _sb_weights_t(za, ca, suffix_t, mask)
            acc_a = acc_a + jnp.dot(vx, wa.astype(BF16), preferred_element_type=F32)
            wb, cb = _sb_weights_t(zb, cb, suffix_t, mask)
            acc_b = acc_b + jnp.dot(vx, wb.astype(BF16), preferred_element_type=F32)
            return ca, acc_a, cb, acc_b

        def kv_block(i, carry):
            kj = qi - 1 - i
            nxt = scores(jnp.maximum(kj - 1, 0))
            return nxt + update(kj, carry[0], carry[1], carry[2:], None)

        c0 = jnp.zeros((1, blk), F32)
        a0 = jnp.zeros((2 * DH_SB, blk), F32)
        nxt = scores(jnp.maximum(qi - 1, 0))
        carry = nxt + update(qi, *scores(qi), (c0, a0, c0, a0), strict)
        carry = lax.fori_loop(0, qi, kv_block, carry)
        _, _, _, acc_a, _, acc_b = carry
        o = jnp.where(upper, acc_a, acc_b)
        o_ref[pl.ds(pl.multiple_of(qi * blk, blk), blk), :] = o.T.astype(o_ref.dtype)
        return 0

    lax.fori_loop(0, nq, q_block, 0)


def _sb_attn_prompt(q16, k16, vt16):
    b, l, dm = q16.shape
    seq = pl.BlockSpec((None, l, 2 * DH_SB), lambda i, h: (i, 0, h))
    return pl.pallas_call(
        _sb_attn_kernel,
        grid=(b, H_SB // 2),
        in_specs=[seq, seq, pl.BlockSpec((2 * DH_SB, l), lambda i, h: (h, i))],
        out_specs=seq,
        out_shape=jax.ShapeDtypeStruct((b, l, dm), BF16),
        scratch_shapes=[pltpu.VMEM((l // ATT_BLOCK, 2 * DH_SB, ATT_BLOCK), BF16)],
        compiler_params=_params("parallel", "parallel"),
        name="sb_attn_prompt",
    )(q16, k16, vt16)


def _sb_dec_kernel(pt_ref, qbd_ref, *refs, past, n_groups, group, n_steps):
    kc_refs, vc_refs = refs[:group], refs[group:2 * group]
    kn_ref, vn_ref, o_ref, carry_ref, acc_ref = refs[2 * group:]
    s_id = pl.program_id(1)
    page = kc_refs[0].shape[0]
    rows = qbd_ref.shape[0]
    suffix = _suffix_matrix(page)

    def process(blocks, masked):
        mask = None
        if masked:
            shape = (rows, page)
            qpos = past + (lax.broadcasted_iota(jnp.int32, shape, 0) >> 4)
            mask = past + lax.broadcasted_iota(jnp.int32, shape, 1) < qpos
        carry = carry_ref[...]
        acc = acc_ref[...]
        for k_ref, v_ref in blocks:
            z = lax.dot_general(qbd_ref[...], k_ref[...].astype(BF16), _NT,
                                preferred_element_type=F32)
            w, carry = _sb_weights(z, carry, suffix, mask)
            acc = acc + jnp.dot(w.astype(BF16), v_ref[...].astype(BF16),
                                preferred_element_type=F32)
        carry_ref[...] = carry
        acc_ref[...] = acc

    @pl.when(s_id == 0)
    def _():
        carry_ref[...] = jnp.zeros(carry_ref.shape, F32)
        acc_ref[...] = jnp.zeros(acc_ref.shape, F32)
        process([(kn_ref, vn_ref)], masked=True)

    @pl.when(s_id > 0)
    def _():
        process([(kc_refs[g], vc_refs[g]) for g in reversed(range(group))], masked=False)

    @pl.when(s_id == n_groups)
    def _():
        acc = acc_ref[...]
        shape = (H_SB, H_SB * DH_SB)
        own = lax.broadcasted_iota(jnp.int32, shape, 1) // DH_SB \
            == lax.broadcasted_iota(jnp.int32, shape, 0)
        for t in range(n_steps):
            o_ref[t:t + 1, :] = jnp.sum(jnp.where(own, acc[t * H_SB:(t + 1) * H_SB], 0.0),
                                        axis=0, keepdims=True)


def _sb_attn_decode(qs16, ks32, vs32, cache_k, cache_v, page_table):
    b, t, dm = qs16.shape
    n_phys, page = cache_k.shape[:2]
    n_pages = page_table.shape[1]
    past = n_pages * page
    assert t * H_SB == 64
    kc = cache_k.reshape(n_phys, page, dm).astype(BF16)
    vc = cache_v.reshape(n_phys, page, dm).astype(BF16)
    col_head = jnp.arange(dm) // DH_SB
    r = jnp.arange(t * H_SB)
    r_t, r_h = r // H_SB, r % H_SB
    sel = col_head[None, :] == r_h[:, None]
    qbd = jnp.where(sel[None], (qs16 * (DH_SB ** -0.5))[:, r_t, :], 0).astype(BF16)
    pad = ((0, 0), (0, page - t), (0, 0))
    kn = jnp.pad(ks32, pad)
    vn = jnp.pad(vs32, pad)
    rows = t * H_SB

    group = DECODE_PAGES_PER_STEP
    n_groups = n_pages // group
    assert n_groups * group == n_pages

    def cache_spec(g):
        def index(i, s, pt):
            return (pt[i, (n_groups - jnp.maximum(s, 1)) * group + g], 0, 0)
        return pl.BlockSpec((None, page, dm), index)

    per_b = lambda i, s, pt: (i, 0, 0)
    grid_spec = pltpu.PrefetchScalarGridSpec(
        num_scalar_prefetch=1,
        grid=(b, n_groups + 1),
        in_specs=[pl.BlockSpec((None, rows, dm), per_b)]
                 + [cache_spec(g) for g in range(group)] * 2
                 + [pl.BlockSpec((None, page, dm), per_b),
                    pl.BlockSpec((None, page, dm), per_b)],
        out_specs=pl.BlockSpec((None, t, dm), per_b),
        scratch_shapes=[pltpu.VMEM((rows, page), F32), pltpu.VMEM((rows, dm), F32)],
    )
    return pl.pallas_call(
        functools.partial(_sb_dec_kernel, past=past, n_groups=n_groups, group=group,
                          n_steps=t),
        grid_spec=grid_spec,
        out_shape=jax.ShapeDtypeStruct((b, t, dm), F32),
        compiler_params=_params("parallel", "arbitrary"),
        name="sb_attn_decode",
    )(page_table, qbd, *([kc] * group), *([vc] * group), kn, vn)


def _load_hist(buf_ref, hist_ref, nrows):
    if hist_ref is None:
        buf_ref[0:nrows, :] = jnp.zeros((nrows, buf_ref.shape[1]), F32)
    else:
        buf_ref[0:nrows, :] = hist_ref[...]


def _pool_kernel(*refs, stride, steps, pos0, has_hist):
    if has_hist:
        x_ref, hist_ref, g_ref, w_ref, sc_ref, o_ref, st_ref, buf_ref = refs
    else:
        x_ref, g_ref, w_ref, sc_ref, o_ref, st_ref, buf_ref = refs
        hist_ref = None
    j = pl.program_id(1)
    rows = steps * stride
    hpad = _hist_pad(POOL_HIST, stride)
    hrows = hpad * stride

    @pl.when(j == 0)
    def _():
        _load_hist(buf_ref, hist_ref, hrows)

    x = x_ref[...]
    u = _rms(x, g_ref[...])
    buf_ref[hrows:hrows + rows, :] = u
    step = lax.broadcasted_iota(jnp.int32, (rows, 1), 0) // stride
    pos = pos0 + j * steps + step
    for g, w in enumerate(POOL_WINDOWS):
        c0, c1 = g * POOL_GROUP, (g + 1) * POOL_GROUP
        ug = u[:, c0:c1]
        win = ug
        for i in range(1, w):
            win = win + buf_ref[(hpad - i) * stride:(hpad - i) * stride + rows, c0:c1]
        cnt = jnp.minimum(pos + 1, w).astype(F32)
        d = win / cnt - ug
        y = jnp.dot(d.astype(BF16), w_ref[g], preferred_element_type=F32) * sc_ref[:, c0:c1]
        o_ref[:, c0:c1] = x[:, c0:c1] + y
    last = buf_ref[rows:rows + hrows, :]
    buf_ref[0:hrows, :] = last
    st_ref[...] = last


def _pool_mixer(x3d, hist, g, w16, scale, *, stride, steps, pos0):
    gdim, total, d = x3d.shape
    rows = steps * stride
    nt = total // rows
    hrows = _hist_pad(POOL_HIST, stride) * stride
    tile = pl.BlockSpec((None, rows, d), lambda i, j: (i, j, 0))
    per_g = pl.BlockSpec((None, hrows, d), lambda i, j: (i, 0, 0))
    const2 = lambda i, j: (0, 0)
    in_specs = [tile] + ([per_g] if hist is not None else []) + [
        pl.BlockSpec((1, d), const2),
        pl.BlockSpec(w16.shape, lambda i, j: (0, 0, 0)),
        pl.BlockSpec((1, d), const2)]
    args = [x3d] + ([hist] if hist is not None else []) + [g.reshape(1, d), w16,
                                                          scale.reshape(1, d)]
    return pl.pallas_call(
        functools.partial(_pool_kernel, stride=stride, steps=steps, pos0=pos0,
                          has_hist=hist is not None),
        grid=(gdim, nt),
        in_specs=in_specs,
        out_specs=[tile, per_g],
        out_shape=[jax.ShapeDtypeStruct(x3d.shape, F32),
                   jax.ShapeDtypeStruct((gdim, hrows, d), F32)],
        scratch_shapes=[pltpu.VMEM((hrows + rows, d), F32)],
        compiler_params=_params("parallel", "arbitrary"),
        name="pool_mixer",
    )(*args)


def _conformer_kernel(*refs, stride, steps, has_hist):
    if has_hist:
        (x_ref, hist_ref, g_ref, w1_ref, b1_ref, wdw_ref, bdw_ref, lng_ref, lnb_ref,
         w2_ref, b2_ref, o_ref, st_ref, buf_ref, conv_ref) = refs
    else:
        (x_ref, g_ref, w1_ref, b1_ref, wdw_ref, bdw_ref, lng_ref, lnb_ref,
         w2_ref, b2_ref, o_ref, st_ref, buf_ref, conv_ref) = refs
        hist_ref = None
    j = pl.program_id(1)
    rows = steps * stride
    hpad = _hist_pad(CONV_WIDTH - 1, stride)
    hrows = hpad * stride
    d = x_ref.shape[-1]

    @pl.when(j == 0)
    def _():
        _load_hist(buf_ref, hist_ref, hrows)

    x = x_ref[...]
    u = _rms(x, g_ref[...]).astype(BF16)
    ag = jnp.dot(u, w1_ref[...], preferred_element_type=F32) + b1_ref[...]
    glu = ag[:, :d] * (1.0 / (1.0 + jnp.exp(-ag[:, d:])))
    buf_ref[hrows:hrows + rows, :] = glu
    for c0 in range(0, d, LANES):
        acc = jnp.zeros((rows, LANES), F32) + bdw_ref[:, c0:c0 + LANES]
        for k in range(CONV_WIDTH):
            off = (hpad - (CONV_WIDTH - 1) + k) * stride
            acc = acc + wdw_ref[k:k + 1, c0:c0 + LANES] * buf_ref[off:off + rows, c0:c0 + LANES]
        conv_ref[:, c0:c0 + LANES] = acc
    c = conv_ref[...]
    mu = jnp.mean(c, axis=-1, keepdims=True)
    cc = c - mu
    var = jnp.mean(cc * cc, axis=-1, keepdims=True)
    c = cc * lax.rsqrt(var + EPS) * lng_ref[...] + lnb_ref[...]
    c = c * (1.0 / (1.0 + jnp.exp(-c)))
    y = jnp.dot(c.astype(BF16), w2_ref[...], preferred_element_type=F32) + b2_ref[...]
    o_ref[...] = x + y
    last = buf_ref[rows:rows + hrows, :]
    buf_ref[0:hrows, :] = last
    st_ref[...] = last


def _conformer(x3d, hist, g, w1_16, b1, wdw, bdw, lng, lnb, w2_16, b2, *, stride, steps):
    gdim, total, d = x3d.shape
    rows = steps * stride
    nt = total // rows
    hrows = _hist_pad(CONV_WIDTH - 1, stride) * stride
    tile = pl.BlockSpec((None, rows, d), lambda i, j: (i, j, 0))
    per_g = pl.BlockSpec((None, hrows, d), lambda i, j: (i, 0, 0))
    const2 = lambda i, j: (0, 0)
    full = lambda a: pl.BlockSpec(a.shape, const2)
    wdw_p = jnp.pad(wdw, ((0, -CONV_WIDTH % SUBLANES), (0, 0)))
    params = [g.reshape(1, d), w1_16, b1.reshape(1, -1), wdw_p, bdw.reshape(1, d),
              lng.reshape(1, d), lnb.reshape(1, d), w2_16, b2.reshape(1, d)]
    in_specs = [tile] + ([per_g] if hist is not None else []) + [full(p) for p in params]
    args = [x3d] + ([hist] if hist is not None else []) + params
    return pl.pallas_call(
        functools.partial(_conformer_kernel, stride=stride, steps=steps,
                          has_hist=hist is not None),
        grid=(gdim, nt),
        in_specs=in_specs,
        out_specs=[tile, per_g],
        out_shape=[jax.ShapeDtypeStruct(x3d.shape, F32),
                   jax.ShapeDtypeStruct((gdim, hrows, d), F32)],
        scratch_shapes=[pltpu.VMEM((hrows + rows, d), F32), pltpu.VMEM((rows, d), F32)],
        compiler_params=_params("parallel", "arbitrary"),
        name="conformer_conv",
    )(*args)


def _ffn_kernel(*refs, stride, steps, has_hist, final_norm):
    refs = list(refs)
    x_ref = refs.pop(0)
    hist_ref = refs.pop(0) if has_hist else None
    g_ref, wup_ref, wdw_ref, wdn_ref = refs[:4]
    refs = refs[4:]
    gf_ref = refs.pop(0) if final_norm else None
    o_ref, st_ref, carry_ref, buf_ref, acc_ref = refs
    j = pl.program_id(1)
    rows = steps * stride
    hrows = _hist_pad(FFN_CONV_WIDTH - 1, stride) * stride
    n_chunks, _, two_tc = wup_ref.shape
    tc = two_tc // 2

    @pl.when(j == 0)
    def _():
        if hist_ref is None:
            carry_ref[...] = jnp.zeros(carry_ref.shape, F32)
        else:
            carry_ref[...] = hist_ref[...]

    x = x_ref[...]
    h = _rms(x, g_ref[...]).astype(BF16)
    for c in range(n_chunks):
        up = jnp.dot(h, wup_ref[c], preferred_element_type=F32)
        buf_ref[0:hrows, :] = carry_ref[c]
        buf_ref[hrows:hrows + rows, :] = up
        w = wdw_ref[c]
        back1 = buf_ref[hrows - stride:hrows - stride + rows, :]
        back2 = buf_ref[hrows - 2 * stride:hrows - 2 * stride + rows, :]
        conv = w[0:1] * back2 + w[1:2] * back1 + w[2:3] * up
        carry_ref[c] = buf_ref[rows:rows + hrows, :]
        gate = conv[:, :tc]
        act = gate * (1.0 / (1.0 + jnp.exp(-gate))) * conv[:, tc:]
        part = jnp.dot(act.astype(BF16), wdn_ref[c], preferred_element_type=F32)
        if c == 0:
            acc_ref[...] = part
        else:
            acc_ref[...] += part
    y = x + acc_ref[...]
    if final_norm:
        y = _rms(y, gf_ref[...])
    o_ref[...] = y

    @pl.when(j == pl.num_programs(1) - 1)
    def _():
        st_ref[...] = carry_ref[...]


def _ffn(x3d, hist, g, wup_c, wdw_c, wdn_c, g_final, *, stride, steps):
    gdim, total, d = x3d.shape
    rows = steps * stride
    nt = total // rows
    hrows = _hist_pad(FFN_CONV_WIDTH - 1, stride) * stride
    n_chunks, _, two_tc = wup_c.shape
    tile = pl.BlockSpec((None, rows, d), lambda i, j: (i, j, 0))
    st_blk = pl.BlockSpec((None, n_chunks, hrows, two_tc), lambda i, j: (i, 0, 0, 0))
    const2 = lambda i, j: (0, 0)
    const3 = lambda i, j: (0, 0, 0)
    in_specs = [tile] + ([st_blk] if hist is not None else []) + [
        pl.BlockSpec((1, d), const2), pl.BlockSpec(wup_c.shape, const3),
        pl.BlockSpec(wdw_c.shape, const3), pl.BlockSpec(wdn_c.shape, const3)]
    args = [x3d] + ([hist] if hist is not None else []) + [g.reshape(1, d), wup_c, wdw_c, wdn_c]
    if g_final is not None:
        in_specs.append(pl.BlockSpec((1, d), const2))
        args.append(g_final.reshape(1, d))
    return pl.pallas_call(
        functools.partial(_ffn_kernel, stride=stride, steps=steps, has_hist=hist is not None,
                          final_norm=g_final is not None),
        grid=(gdim, nt),
        in_specs=in_specs,
        out_specs=[tile, st_blk],
        out_shape=[jax.ShapeDtypeStruct(x3d.shape, F32),
                   jax.ShapeDtypeStruct((gdim, n_chunks, hrows, two_tc), F32)],
        scratch_shapes=[pltpu.VMEM((n_chunks, hrows, two_tc), F32),
                        pltpu.VMEM((hrows + rows, two_tc), F32),
                        pltpu.VMEM((rows, d), F32)],
        compiler_params=_params("parallel", "arbitrary"),
        name="conv_ffn",
    )(*args)


def _ffn_weights(w_up, w_dw, w_down):
    d, two_f = w_up.shape
    f = two_f // 2
    tc = FFN_CHUNK
    nc = f // tc
    assert nc * tc == f
    wup_c = w_up.reshape(d, 2, nc, tc).transpose(2, 0, 1, 3).reshape(nc, d, 2 * tc).astype(BF16)
    wdw_c = w_dw.reshape(FFN_CONV_WIDTH, 2, nc, tc).transpose(2, 0, 1, 3).reshape(
        nc, FFN_CONV_WIDTH, 2 * tc)
    wdn_c = w_down.reshape(nc, tc, w_down.shape[1]).astype(BF16)
    return wup_c, wdw_c, wdn_c


def _ffn_state_from_chunks(st, stride):
    gdim, nc, _, two_tc = st.shape
    keep = (FFN_CONV_WIDTH - 1) * stride
    st = st[:, :, -keep:, :].reshape(gdim, nc, keep, 2, two_tc // 2)
    return st.transpose(0, 2, 3, 1, 4).reshape(gdim, keep, nc * two_tc)


def _ffn_hist_to_chunks(hist_rows, nc):
    r, two_f = hist_rows.shape
    stride = r // (FFN_CONV_WIDTH - 1)
    tc = two_f // 2 // nc
    hrows = _hist_pad(FFN_CONV_WIDTH - 1, stride) * stride
    h = jnp.pad(hist_rows, ((hrows - r, 0), (0, 0)))
    h = h.reshape(hrows, 2, nc, tc).transpose(2, 0, 1, 3)
    return h.reshape(1, nc, hrows, 2 * tc)


def _pad_hist(hist_rows, need, stride):
    extra = (_hist_pad(need, stride) - need) * stride
    return jnp.pad(hist_rows, ((extra, 0), (0, 0)))


def _time_major(a):
    a = jnp.swapaxes(a, 0, 1)
    return a.reshape(a.shape[0] * a.shape[1], *a.shape[2:])


def _batch_major(a, nb):
    a = a.reshape(a.shape[0] // nb, nb, *a.shape[1:])
    return jnp.swapaxes(a, 0, 1)


def kernel(x_prompt, x_sample, cache_k_diff, cache_v_diff, cache_k_sb, cache_v_sb, state_pool, state_conv, state_ffn, page_table, rel_bias, norm_mix, norm_ffn, norm_final, diff_w_qkv, diff_w_o, diff_lambda_q1, diff_lambda_k1, diff_lambda_q2, diff_lambda_k2, diff_subln, sb_w_qkv, sb_w_o, pool_w, pool_scale, conv_w_pw1, conv_b_pw1, conv_w_dw, conv_b_dw, conv_ln_g, conv_ln_b, conv_w_pw2, conv_b_pw2, ffn_w_up, ffn_w_dw, ffn_w_down):
    bp, lp, d = x_prompt.shape
    bs, ls, _ = x_sample.shape
    depth = ffn_w_up.shape[0]
    past = page_table.shape[1] * cache_k_diff.shape[1]
    seq_tile = min(lp, 256)

    xp = x_prompt
    xs = _time_major(x_sample)[None]
    ffn_p, ffn_s = [], []
    outs = {}
    for layer in range(depth):
        kind = layer % 4
        if kind == 0:
            lam_init = 0.8 - 0.6 * math.exp(-0.3 * layer)
            w16 = diff_w_qkv.astype(BF16)
            wo16 = diff_w_o.astype(BF16)
            lams = (diff_lambda_q1, diff_lambda_k1, diff_lambda_q2, diff_lambda_k2)
            wvt16 = diff_w_qkv[:, 2 * d:].T.astype(BF16)
            q16, k32, v32, k16, vt16 = _qkv_proj(xp.reshape(bp * lp, d), norm_mix[layer],
                                                 w16, wvt16)
            shp = (bp, lp, d)
            attn = _diff_attn_prompt(q16.reshape(shp), k16.reshape(shp), vt16,
                                     _bias_table(rel_bias), *lams, diff_subln, lam_init)
            xp = _mm_resid(attn.reshape(bp * lp, d), wo16, xp.reshape(bp * lp, d)).reshape(shp)
            outs['k_diff_p'] = k32.reshape(bp, lp, H_DIFF, 2 * DH_DIFF)
            outs['v_diff_p'] = v32.reshape(bp, lp, H_DIFF, 2 * DH_DIFF)

            q16, k32, v32 = _qkv_proj(xs[0], norm_mix[layer], w16)
            qs, ks, vs = (_batch_major(a, bs) for a in (q16, k32, v32))
            attn = _diff_attn_decode(qs, ks, vs, cache_k_diff, cache_v_diff, page_table,
                                     rel_bias, *lams, diff_subln, lam_init)
            xs = _mm_resid(_time_major(attn.reshape(bs, ls, d)), wo16, xs[0])[None]
            outs['k_diff_s'] = ks.reshape(bs, ls, H_DIFF, 2 * DH_DIFF)
            outs['v_diff_s'] = vs.reshape(bs, ls, H_DIFF, 2 * DH_DIFF)
        elif kind == 1:
            w16 = sb_w_qkv.astype(BF16)
            wo16 = sb_w_o.astype(BF16)
            wvt16 = sb_w_qkv[:, 2 * d:].T.astype(BF16)
            q16, k32, v32, k16, vt16 = _qkv_proj(xp.reshape(bp * lp, d), norm_mix[layer],
                                                 w16, wvt16)
            shp = (bp, lp, d)
            attn = _sb_attn_prompt(q16.reshape(shp), k16.reshape(shp), vt16)
            xp = _mm_resid(attn.reshape(bp * lp, d), wo16, xp.reshape(bp * lp, d)).reshape(shp)
            outs['k_sb_p'] = k32.reshape(bp, lp, H_SB, DH_SB)
            outs['v_sb_p'] = v32.reshape(bp, lp, H_SB, DH_SB)

            q16, k32, v32 = _qkv_proj(xs[0], norm_mix[layer], w16)
            qs, ks, vs = (_batch_major(a, bs) for a in (q16, k32, v32))
            attn = _sb_attn_decode(qs, ks, vs, cache_k_sb, cache_v_sb, page_table)
            xs = _mm_resid(_time_major(attn), wo16, xs[0])[None]
            outs['k_sb_s'] = ks.reshape(bs, ls, H_SB, DH_SB)
            outs['v_sb_s'] = vs.reshape(bs, ls, H_SB, DH_SB)
        elif kind == 2:
            pw16 = pool_w.astype(BF16)
            xp, st = _pool_mixer(xp, None, norm_mix[layer], pw16, pool_scale,
                                 stride=1, steps=seq_tile, pos0=0)
            outs['pool_p'] = st[:, -POOL_HIST:]
            hist = _pad_hist(_time_major(state_pool), POOL_HIST, bs)[None]
            xs, st = _pool_mixer(xs, hist, norm_mix[layer], pw16, pool_scale,
                                 stride=bs, steps=ls, pos0=past)
            outs['pool_s'] = _batch_major(st[0, -bs * POOL_HIST:], bs)
        else:
            w1 = conv_w_pw1.astype(BF16)
            w2 = conv_w_pw2.astype(BF16)
            cargs = (norm_mix[layer], w1, conv_b_pw1, conv_w_dw, conv_b_dw, conv_ln_g,
                     conv_ln_b, w2, conv_b_pw2)
            keep = CONV_WIDTH - 1
            xp, st = _conformer(xp, None, *cargs, stride=1, steps=seq_tile)
            outs['conv_p'] = st[:, -keep:]
            hist = _pad_hist(_time_major(state_conv), keep, bs)[None]
            xs, st = _conformer(xs, hist, *cargs, stride=bs, steps=ls)
            outs['conv_s'] = _batch_major(st[0, -bs * keep:], bs)

        wup_c, wdw_c, wdn_c = _ffn_weights(ffn_w_up[layer], ffn_w_dw[layer], ffn_w_down[layer])
        g_final = norm_final if layer == depth - 1 else None
        xp, st = _ffn(xp, None, norm_ffn[layer], wup_c, wdw_c, wdn_c, g_final,
                      stride=1, steps=seq_tile)
        ffn_p.append(_ffn_state_from_chunks(st, 1))
        hist = _ffn_hist_to_chunks(_time_major(state_ffn[layer]), wup_c.shape[0])
        xs, st = _ffn(xs, hist, norm_ffn[layer], wup_c, wdw_c, wdn_c, g_final,
                      stride=bs, steps=ls)
        ffn_s.append(_batch_major(_ffn_state_from_chunks(st, bs)[0], bs))

    y_prompt = xp
    y_sample = _batch_major(xs[0], bs)
    return (y_prompt, y_sample,
            outs['k_diff_p'], outs['v_diff_p'], outs['k_sb_p'], outs['v_sb_p'],
            outs['pool_p'], outs['conv_p'], jnp.stack(ffn_p, axis=0),
            outs['k_diff_s'], outs['v_diff_s'], outs['k_sb_s'], outs['v_sb_s'],
            outs['pool_s'], outs['conv_s'], jnp.stack(ffn_s, axis=0))
```

```python
import functools
import math

import jax
import jax.numpy as jnp
from jax import lax
from jax.experimental import pallas as pl
from jax.experimental.pallas import tpu as pltpu

D_MODEL = 1024
H_DIFF = 8
DH_DIFF = 64
H_SB = 16
DH_SB = 64
N_BUCKETS = 32
MAX_DISTANCE = 128
POOL_WINDOWS = (2, 4, 8, 16)
POOL_GROUP = D_MODEL // len(POOL_WINDOWS)
POOL_HIST = max(POOL_WINDOWS) - 1
CONV_WIDTH = 31
FFN_CONV_WIDTH = 3
EPS = 1e-6
NEG_INF = -1e30

BF16 = jnp.bfloat16
F32 = jnp.float32

V7X_VMEM_BYTES = 64 * 1024 * 1024
VMEM_LIMIT = V7X_VMEM_BYTES - 12 * 1024 * 1024
SUBLANES = 8
LANES = 128

BF16_ROWS = 16
ATT_BLOCK = 256
FFN_CHUNK = 256
DECODE_PAGES_PER_STEP = 4


def _hist_pad(need, stride):
    h = need
    while (h * stride) % SUBLANES:
        h += 1
    return h

_NT = (((1,), (1,)), ((), ()))


def _params(*sem):
    return pltpu.CompilerParams(dimension_semantics=sem, vmem_limit_bytes=VMEM_LIMIT)


def _rms(x, g):
    return x * lax.rsqrt(jnp.mean(x * x, axis=-1, keepdims=True) + EPS) * g


def _row_tile(m, want):
    t = min(m, want)
    assert m % t == 0, (m, t)
    return t


def _qkv_kernel(x_ref, g_ref, w_ref, *refs, with_vt):
    if with_vt:
        wvt_ref, q16_ref, k32_ref, v32_ref, k16_ref, vt16_ref = refs
    else:
        q16_ref, k32_ref, v32_ref = refs
    n = q16_ref.shape[-1]
    u = _rms(x_ref[...], g_ref[...]).astype(BF16)
    q = jnp.dot(u, w_ref[:, 0:n], preferred_element_type=F32)
    q16_ref[...] = q.astype(BF16)
    k = jnp.dot(u, w_ref[:, n:2 * n], preferred_element_type=F32)
    k32_ref[...] = k
    v32_ref[...] = jnp.dot(u, w_ref[:, 2 * n:3 * n], preferred_element_type=F32)
    if with_vt:
        k16_ref[...] = k.astype(BF16)
        vt = lax.dot_general(wvt_ref[...], u, _NT, preferred_element_type=F32)
        vt16_ref[...] = vt.astype(BF16)


def _qkv_proj(x2d, g, w16, wvt16=None):
    m, d = x2d.shape
    n = w16.shape[1] // 3
    tm = _row_tile(m, 256)
    row = lambda i: (i, 0)
    const = lambda i: (0, 0)
    out_blk = pl.BlockSpec((tm, n), row)
    in_specs = [pl.BlockSpec((tm, d), row), pl.BlockSpec((1, d), const),
                pl.BlockSpec((d, 3 * n), const)]
    out_specs = [out_blk] * 3
    out_shape = [jax.ShapeDtypeStruct((m, n), BF16), jax.ShapeDtypeStruct((m, n), F32),
                 jax.ShapeDtypeStruct((m, n), F32)]
    args = [x2d, g.reshape(1, d), w16]
    if wvt16 is not None:
        in_specs.append(pl.BlockSpec((n, d), const))
        out_specs += [out_blk, pl.BlockSpec((n, tm), lambda i: (0, i))]
        out_shape += [jax.ShapeDtypeStruct((m, n), BF16), jax.ShapeDtypeStruct((n, m), BF16)]
        args.append(wvt16)
    return pl.pallas_call(
        functools.partial(_qkv_kernel, with_vt=wvt16 is not None),
        grid=(m // tm,),
        in_specs=in_specs,
        out_specs=out_specs,
        out_shape=out_shape,
        compiler_params=_params("parallel"),
        name="qkv_proj",
    )(*args)


def _mm_resid_kernel(a_ref, w_ref, r_ref, o_ref):
    o_ref[...] = r_ref[...] + jnp.dot(a_ref[...].astype(BF16), w_ref[...],
                                      preferred_element_type=F32)


def _mm_resid(a2d, w16, resid2d):
    m, k = a2d.shape
    n = w16.shape[1]
    tm = _row_tile(m, 512)
    row = lambda i: (i, 0)
    return pl.pallas_call(
        _mm_resid_kernel,
        grid=(m // tm,),
        in_specs=[pl.BlockSpec((tm, k), row), pl.BlockSpec((k, n), lambda i: (0, 0)),
                  pl.BlockSpec((tm, n), row)],
        out_specs=pl.BlockSpec((tm, n), row),
        out_shape=jax.ShapeDtypeStruct((m, n), F32),
        compiler_params=_params("parallel"),
        name="out_proj_resid",
    )(a2d, w16, resid2d)


def _t5_bucket(rel):
    n = jnp.maximum(rel, 0)
    max_exact = N_BUCKETS // 2
    nf = jnp.maximum(n, 1).astype(F32)
    large = max_exact + (jnp.log(nf / max_exact) / math.log(MAX_DISTANCE / max_exact)
                         * (N_BUCKETS - max_exact)).astype(jnp.int32)
    large = jnp.minimum(large, N_BUCKETS - 1)
    return jnp.where(n < max_exact, n, large)


def _bias_table_kernel(rb_ref, o_ref):
    d = pl.program_id(0)
    shape = (ATT_BLOCK, ATT_BLOCK)
    rel = d * ATT_BLOCK + lax.broadcasted_iota(jnp.int32, shape, 1) \
        - lax.broadcasted_iota(jnp.int32, shape, 0)
    bucket = _t5_bucket(rel)
    for h in range(H_DIFF):
        val = jnp.zeros(shape, F32)
        for b in range(N_BUCKETS):
            val = jnp.where(bucket == b, rb_ref[b, h], val)
        o_ref[h] = jnp.where(rel >= 0, val, NEG_INF)


def _bias_table(rel_bias):
    assert ATT_BLOCK >= MAX_DISTANCE
    return pl.pallas_call(
        _bias_table_kernel,
        grid=(3,),
        in_specs=[pl.BlockSpec(memory_space=pltpu.SMEM)],
        out_specs=pl.BlockSpec((None, H_DIFF, ATT_BLOCK, ATT_BLOCK), lambda d: (d, 0, 0, 0)),
        out_shape=jax.ShapeDtypeStruct((3, H_DIFF, ATT_BLOCK, ATT_BLOCK), F32),
        compiler_params=_params("parallel"),
        name="t5_bias_table",
    )(rel_bias)


def _diff_lambda(lq1, lk1, lq2, lk2, lam_init):
    e1 = jnp.exp(jnp.sum(lq1 * lk1, axis=-1, keepdims=True))
    e2 = jnp.exp(jnp.sum(lq2 * lk2, axis=-1, keepdims=True))
    return e1 - e2 + lam_init


def _load_vt_chunks(vt_ref, vx_ref, ones_rows):
    blk = ATT_BLOCK
    hd = vt_ref.shape[0]
    for j in range(vx_ref.shape[0]):
        vx_ref[j, 0:hd, :] = vt_ref[:, j * blk:(j + 1) * blk]
        if ones_rows:
            vx_ref[j, hd:hd + ones_rows, :] = jnp.ones((ones_rows, blk), BF16)


def _diff_attn_kernel(q_ref, k_ref, vt_ref, bias_ref, lq1_ref, lk1_ref, lq2_ref, lk2_ref,
                      subln_ref, o_ref, vx_ref, *, lam_init):
    blk = ATT_BLOCK
    hd = 2 * DH_DIFF
    nq = q_ref.shape[0] // blk
    first = lax.broadcasted_iota(jnp.int32, (blk, hd), 1) < DH_DIFF
    lam = _diff_lambda(lq1_ref[...], lk1_ref[...], lq2_ref[...], lk2_ref[...], lam_init)
    subln = subln_ref[...] * (1.0 - lam_init)
    scale = DH_DIFF ** -0.5
    _load_vt_chunks(vt_ref, vx_ref, BF16_ROWS)

    def q_block(qi, _):
        q = q_ref[pl.ds(pl.multiple_of(qi * blk, blk), blk), :] * scale
        zero = jnp.zeros_like(q)
        qa = jnp.where(first, q, zero)
        qb = jnp.where(first, zero, q)

        def scores(kj):
            ks = k_ref[pl.ds(pl.multiple_of(kj * blk, blk), blk), :]
            bias = bias_ref[jnp.minimum(qi - kj, 2)]
            return (lax.dot_general(ks, qa, _NT, preferred_element_type=F32) + bias,
                    lax.dot_general(ks, qb, _NT, preferred_element_type=F32) + bias)

        def update(kj, s, m, a):
            mn = jnp.maximum(m, jnp.max(s, axis=0, keepdims=True))
            alpha = jnp.exp(m - mn)
            p = jnp.exp(s - mn)
            a = alpha * a + jnp.dot(vx_ref[kj], p.astype(BF16), preferred_element_type=F32)
            return mn, a

        def kv_block(kj, carry):
            s1, s2, m1, a1, m2, a2 = carry
            n1, n2 = scores(kj + 1)
            m1, a1 = update(kj, s1, m1, a1)
            m2, a2 = update(kj, s2, m2, a2)
            return n1, n2, m1, a1, m2, a2

        m0 = jnp.full((1, blk), NEG_INF, F32)
        a0 = jnp.zeros((hd + BF16_ROWS, blk), F32)
        s1, s2, m1, a1, m2, a2 = lax.fori_loop(0, qi, kv_block, scores(0) + (m0, a0, m0, a0))
        m1, a1 = update(qi, s1, m1, a1)
        m2, a2 = update(qi, s2, m2, a2)
        o = a1[:hd] * (1.0 / a1[hd:hd + 1]) - lam * (a2[:hd] * (1.0 / a2[hd:hd + 1]))
        o = o * lax.rsqrt(jnp.mean(o * o, axis=0, keepdims=True) + EPS)
        o_ref[pl.ds(pl.multiple_of(qi * blk, blk), blk), :] = (o.T * subln).astype(o_ref.dtype)
        return 0

    lax.fori_loop(0, nq, q_block, 0)


def _diff_attn_prompt(q16, k16, vt16, bias_tab, lq1, lk1, lq2, lk2, subln, lam_init):
    b, l, _ = q16.shape
    hd = 2 * DH_DIFF
    seq = pl.BlockSpec((None, l, hd), lambda i, h: (i, 0, h))
    vec = lambda n: pl.BlockSpec((1, n), lambda i, h: (0, 0))
    return pl.pallas_call(
        functools.partial(_diff_attn_kernel, lam_init=lam_init),
        grid=(b, H_DIFF),
        in_specs=[seq, seq, pl.BlockSpec((hd, l), lambda i, h: (h, i)),
                  pl.BlockSpec((3, None, ATT_BLOCK, ATT_BLOCK), lambda i, h: (0, h, 0, 0)),
                  vec(DH_DIFF), vec(DH_DIFF), vec(DH_DIFF), vec(DH_DIFF), vec(hd)],
        out_specs=seq,
        out_shape=jax.ShapeDtypeStruct((b, l, H_DIFF * hd), BF16),
        scratch_shapes=[pltpu.VMEM((l // ATT_BLOCK, hd + BF16_ROWS, ATT_BLOCK), BF16)],
        compiler_params=_params("parallel", "parallel"),
        name="diff_attn_prompt",
    )(q16, k16, vt16, bias_tab, lq1.reshape(1, -1), lk1.reshape(1, -1), lq2.reshape(1, -1),
      lk2.reshape(1, -1), subln.reshape(1, -1))


def _diff_dec_kernel(pt_ref, qm_ref, *refs, past, n_groups, group, n_steps, lam_init):
    kc_refs, vc_refs = refs[:group], refs[group:2 * group]
    (kn_ref, vn_ref, tab_ref, lq1_ref, lk1_ref, lq2_ref, lk2_ref, subln_ref, o_ref,
     m_ref, l_ref, acc_ref, far_ref) = refs[2 * group:]
    s_id = pl.program_id(1)
    page = kc_refs[0].shape[0]
    rows = qm_ref.shape[0]
    hd = 2 * DH_DIFF
    last_bucket = tab_ref[:, N_BUCKETS - 1:N_BUCKETS]

    def own_head(shape):
        return (lax.broadcasted_iota(jnp.int32, shape, 1) & (H_DIFF - 1)) \
            == (lax.broadcasted_iota(jnp.int32, shape, 0) & (H_DIFF - 1))

    @pl.when(s_id == 0)
    def _():
        m_ref[...] = jnp.full(m_ref.shape, NEG_INF, F32)
        l_ref[...] = jnp.zeros(l_ref.shape, F32)
        acc_ref[...] = jnp.zeros(acc_ref.shape, F32)
        far_ref[...] = jnp.where(own_head(far_ref.shape), last_bucket, NEG_INF)

    def near_term(keys, kpos0):
        shape = (rows, keys * H_DIFF)
        step = (lax.broadcasted_iota(jnp.int32, shape, 0) >> 3) & (n_steps - 1)
        rel = (past + step) - (kpos0 + (lax.broadcasted_iota(jnp.int32, shape, 1) >> 3))
        bucket = _t5_bucket(rel)
        bias = jnp.zeros(shape, F32)
        for b in range(N_BUCKETS):
            bias = jnp.where(bucket == b, tab_ref[:, b:b + 1], bias)
        return jnp.where(jnp.logical_and(own_head(shape), rel >= 0), bias, NEG_INF)

    def process(blocks):
        scores, values = [], []
        for k_ref, v_ref, term in blocks:
            keys = k_ref.shape[0]
            kb = k_ref[...].reshape(keys * H_DIFF, hd).astype(BF16)
            values.append(v_ref[...].reshape(keys * H_DIFF, hd).astype(BF16))
            scores.append(lax.dot_general(qm_ref[...], kb, _NT, preferred_element_type=F32)
                          + term)
        m = m_ref[...]
        mn = m
        for s in scores:
            mn = jnp.maximum(mn, jnp.max(s, axis=-1, keepdims=True))
        alpha = jnp.exp(m - mn)
        l = alpha * l_ref[...]
        acc = alpha * acc_ref[...]
        for s, vb in zip(scores, values):
            p = jnp.exp(s - mn)
            l = l + jnp.sum(p, axis=-1, keepdims=True)
            acc = acc + jnp.dot(p.astype(BF16), vb, preferred_element_type=F32)
        l_ref[...] = l
        acc_ref[...] = acc
        m_ref[...] = mn

    @pl.when(s_id < n_groups)
    def _():
        blocks = []
        for g in range(group):
            kpos0 = (s_id * group + g) * page
            far = kpos0 + page + MAX_DISTANCE <= past
            term = lax.cond(far, lambda: far_ref[...],
                            functools.partial(near_term, page, kpos0))
            blocks.append((kc_refs[g], vc_refs[g], term))
        process(blocks)

    @pl.when(s_id == n_groups)
    def _():
        process([(kn_ref, vn_ref, near_term(kn_ref.shape[0], past))])
        lam = _diff_lambda(lq1_ref[...], lk1_ref[...], lq2_ref[...], lk2_ref[...], lam_init)
        subln = subln_ref[...] * (1.0 - lam_init)
        accn = acc_ref[...] * (1.0 / l_ref[...])
        half = n_steps * H_DIFF
        o = accn[:half] - lam * accn[half:]
        o_ref[...] = _rms(o, subln).reshape(o_ref.shape)


def _diff_attn_decode(qs16, ks32, vs32, cache_k, cache_v, page_table, rel_bias,
                      lq1, lk1, lq2, lk2, subln, lam_init):
    b, t, dm = qs16.shape
    n_phys, page = cache_k.shape[:2]
    n_pages = page_table.shape[1]
    past = n_pages * page
    hd = 2 * DH_DIFF
    assert t & (t - 1) == 0 and t <= SUBLANES and cache_k.shape[2:] == (H_DIFF, hd)
    half_c = (jnp.arange(hd) // DH_DIFF)[None, :] == jnp.arange(2)[:, None]
    q4 = (qs16 * (DH_DIFF ** -0.5)).reshape(b, 1, t, H_DIFF, hd)
    qm = jnp.where(half_c[None, :, None, None, :], q4, 0).astype(BF16)
    rows = 2 * t * H_DIFF
    qm = qm.reshape(b, rows, hd)
    pad = ((0, 0), (0, SUBLANES - t), (0, 0), (0, 0))
    kn = jnp.pad(ks32.reshape(b, t, H_DIFF, hd), pad)
    vn = jnp.pad(vs32.reshape(b, t, H_DIFF, hd), pad)
    tab = jnp.tile(rel_bias.T, (2 * t, 1))

    group = DECODE_PAGES_PER_STEP
    n_groups = n_pages // group
    assert n_groups * group == n_pages

    def cache_spec(g):
        def index(i, s, pt):
            return (pt[i, jnp.minimum(s, n_groups - 1) * group + g], 0, 0, 0)
        return pl.BlockSpec((None, page, H_DIFF, hd), index)

    per_b3 = lambda i, s, pt: (i, 0, 0)
    per_b4 = lambda i, s, pt: (i, 0, 0, 0)
    vec = lambda n: pl.BlockSpec((1, n), lambda i, s, pt: (0, 0))
    grid_spec = pltpu.PrefetchScalarGridSpec(
        num_scalar_prefetch=1,
        grid=(b, n_groups + 1),
        in_specs=[pl.BlockSpec((None, rows, hd), per_b3)]
                 + [cache_spec(g) for g in range(group)] * 2
                 + [pl.BlockSpec((None, SUBLANES, H_DIFF, hd), per_b4),
                  pl.BlockSpec((None, SUBLANES, H_DIFF, hd), per_b4),
                  pl.BlockSpec((rows, N_BUCKETS), lambda i, s, pt: (0, 0)),
                  vec(DH_DIFF), vec(DH_DIFF), vec(DH_DIFF), vec(DH_DIFF), vec(hd)],
        out_specs=pl.BlockSpec((None, t, H_DIFF, hd), per_b4),
        scratch_shapes=[pltpu.VMEM((rows, 1), F32), pltpu.VMEM((rows, 1), F32),
                        pltpu.VMEM((rows, hd), F32),
                        pltpu.VMEM((rows, page * H_DIFF), F32)],
    )
    return pl.pallas_call(
        functools.partial(_diff_dec_kernel, past=past, n_groups=n_groups, group=group,
                          n_steps=t, lam_init=lam_init),
        grid_spec=grid_spec,
        out_shape=jax.ShapeDtypeStruct((b, t, H_DIFF, hd), F32),
        compiler_params=_params("parallel", "arbitrary"),
        name="diff_attn_decode",
    )(page_table, qm, *([cache_k] * group), *([cache_v] * group), kn, vn, tab,
      lq1.reshape(1, -1), lk1.reshape(1, -1), lq2.reshape(1, -1), lk2.reshape(1, -1),
      subln.reshape(1, -1))


def _suffix_matrix(n):
    r = lax.broadcasted_iota(jnp.int32, (n, 2 * n), 0)
    c = lax.broadcasted_iota(jnp.int32, (n, 2 * n), 1)
    return jnp.where(jnp.logical_or(c >= n, r > c), 1.0, 0.0).astype(BF16)


def _sb_weights(z, carry, suffix, mask):
    n = z.shape[1]
    e = jnp.exp(-jnp.abs(z))
    ls = jnp.minimum(z, 0.0) - jnp.log(1.0 + e)
    lk = ls - z
    if mask is not None:
        lk = jnp.where(mask, lk, 0.0)
    hi = lk.astype(BF16)
    lo = (lk - hi.astype(F32)).astype(BF16)
    t = jnp.dot(jnp.concatenate([hi, lo], axis=0), suffix, preferred_element_type=F32)
    t = t[:z.shape[0]] + t[z.shape[0]:]
    a = jnp.exp(ls + t[:, :n] + carry)
    if mask is not None:
        a = jnp.where(mask, a, 0.0)
    return a, carry + t[:, n:]


def _sb_weights_t(z, carry, suffix_t, mask):
    n = z.shape[0]
    e = jnp.exp(-jnp.abs(z))
    ls = jnp.minimum(z, 0.0) - jnp.log(1.0 + e)
    lk = ls - z
    if mask is not None:
        lk = jnp.where(mask, lk, 0.0)
    t = jnp.dot(suffix_t, lk.astype(BF16), preferred_element_type=F32)
    a = jnp.exp(ls + t[:n] + carry)
    if mask is not None:
        a = jnp.where(mask, a, 0.0)
    return a, carry + t[n:n + 1]


def _sb_attn_kernel(q_ref, k_ref, vt_ref, o_ref, vx_ref):
    blk = ATT_BLOCK
    nq = q_ref.shape[0] // blk
    first = lax.broadcasted_iota(jnp.int32, (blk, 2 * DH_SB), 1) < DH_SB
    rows = lax.broadcasted_iota(jnp.int32, (blk + BF16_ROWS, blk), 0)
    cols = lax.broadcasted_iota(jnp.int32, (blk + BF16_ROWS, blk), 1)
    suffix_t = jnp.where(jnp.logical_or(rows >= blk, cols > rows), 1.0, 0.0).astype(BF16)
    strict = lax.broadcasted_iota(jnp.int32, (blk, blk), 0) \
        < lax.broadcasted_iota(jnp.int32, (blk, blk), 1)
    upper = lax.broadcasted_iota(jnp.int32, (2 * DH_SB, blk), 0) < DH_SB
    scale = DH_SB ** -0.5
    _load_vt_chunks(vt_ref, vx_ref, 0)

    def q_block(qi, _):
        q = q_ref[pl.ds(pl.multiple_of(qi * blk, blk), blk), :] * scale
        zero = jnp.zeros_like(q)
        qa = jnp.where(first, q, zero)
        qb = jnp.where(first, zero, q)

        def scores(kj):
            ks = k_ref[pl.ds(pl.multiple_of(kj * blk, blk), blk), :]
            return (lax.dot_general(ks, qa, _NT, preferred_element_type=F32),
                    lax.dot_general(ks, qb, _NT, preferred_element_type=F32))

        def update(kj, za, zb, state, mask):
            ca, acc_a, cb, acc_b = state
            vx = vx_ref[kj]
            wa, ca = _sb_weights_t(za, ca, suffix_t, mask)
            acc_a = acc_a + jnp.dot(vx, wa.astype(BF16), preferred_element_type=F32)
            wb, cb = _sb_weights_t(zb, cb, suffix_t, mask)
            acc_b = acc_b + jnp.dot(vx, wb.astype(BF16), preferred_element_type=F32)
            return ca, acc_a, cb, acc_b

        def kv_block(i, carry):
            kj = qi - 1 - i
            nxt = scores(jnp.maximum(kj - 1, 0))
            return nxt + update(kj, carry[0], carry[1], carry[2:], None)

        c0 = jnp.zeros((1, blk), F32)
        a0 = jnp.zeros((2 * DH_SB, blk), F32)
        nxt = scores(jnp.maximum(qi - 1, 0))
        carry = nxt + update(qi, *scores(qi), (c0, a0, c0, a0), strict)
        carry = lax.fori_loop(0, qi, kv_block, carry)
        _, _, _, acc_a, _, acc_b = carry
        o = jnp.where(upper, acc_a, acc_b)
        o_ref[pl.ds(pl.multiple_of(qi * blk, blk), blk), :] = o.T.astype(o_ref.dtype)
        return 0

    lax.fori_loop(0, nq, q_block, 0)


def _sb_attn_prompt(q16, k16, vt16):
    b, l, dm = q16.shape
    seq = pl.BlockSpec((None, l, 2 * DH_SB), lambda i, h: (i, 0, h))
    return pl.pallas_call(
        _sb_attn_kernel,
        grid=(b, H_SB // 2),
        in_specs=[seq, seq, pl.BlockSpec((2 * DH_SB, l), lambda i, h: (h, i))],
        out_specs=seq,
        out_shape=jax.ShapeDtypeStruct((b, l, dm), BF16),
        scratch_shapes=[pltpu.VMEM((l // ATT_BLOCK, 2 * DH_SB, ATT_BLOCK), BF16)],
        compiler_params=_params("parallel", "parallel"),
        name="sb_attn_prompt",
    )(q16, k16, vt16)


def _sb_dec_kernel(pt_ref, qbd_ref, *refs, past, n_groups, group, n_steps):
    kc_refs, vc_refs = refs[:group], refs[group:2 * group]
    kn_ref, vn_ref, o_ref, carry_ref, acc_ref = refs[2 * group:]
    s_id = pl.program_id(1)
    page = kn_ref.shape[0]
    rows = qbd_ref.shape[0]
    suffix = _suffix_matrix(page)

    def dense(ref):
        return ref[...].astype(BF16)

    def from_cache(ref):
        heads = [ref[pl.ds(h, page, stride=H_SB), :] for h in range(H_SB)]
        return jnp.concatenate(heads, axis=-1).astype(BF16)

    def process(blocks, load, masked):
        mask = None
        if masked:
            shape = (rows, page)
            qpos = past + (lax.broadcasted_iota(jnp.int32, shape, 0) >> 4)
            mask = past + lax.broadcasted_iota(jnp.int32, shape, 1) < qpos
        carry = carry_ref[...]
        acc = acc_ref[...]
        for k_ref, v_ref in blocks:
            z = lax.dot_general(qbd_ref[...], load(k_ref), _NT, preferred_element_type=F32)
            w, carry = _sb_weights(z, carry, suffix, mask)
            acc = acc + jnp.dot(w.astype(BF16), load(v_ref), preferred_element_type=F32)
        carry_ref[...] = carry
        acc_ref[...] = acc

    @pl.when(s_id == 0)
    def _():
        carry_ref[...] = jnp.zeros(carry_ref.shape, F32)
        acc_ref[...] = jnp.zeros(acc_ref.shape, F32)
        process([(kn_ref, vn_ref)], dense, masked=True)

    @pl.when(s_id > 0)
    def _():
        process([(kc_refs[g], vc_refs[g]) for g in reversed(range(group))], from_cache,
                masked=False)

    @pl.when(s_id == n_groups)
    def _():
        acc = acc_ref[...]
        shape = (H_SB, H_SB * DH_SB)
        own = lax.broadcasted_iota(jnp.int32, shape, 1) // DH_SB \
            == lax.broadcasted_iota(jnp.int32, shape, 0)
        for t in range(n_steps):
            o_ref[t:t + 1, :] = jnp.sum(jnp.where(own, acc[t * H_SB:(t + 1) * H_SB], 0.0),
                                        axis=0, keepdims=True)


def _sb_attn_decode(qs16, ks32, vs32, cache_k, cache_v, page_table):
    b, t, dm = qs16.shape
    n_phys, page = cache_k.shape[:2]
    n_pages = page_table.shape[1]
    past = n_pages * page
    assert t * H_SB == 64
    kc = cache_k.reshape(n_phys, page * H_SB, DH_SB)
    vc = cache_v.reshape(n_phys, page * H_SB, DH_SB)
    col_head = jnp.arange(dm) // DH_SB
    r = jnp.arange(t * H_SB)
    r_t, r_h = r // H_SB, r % H_SB
    sel = col_head[None, :] == r_h[:, None]
    qbd = jnp.where(sel[None], (qs16 * (DH_SB ** -0.5))[:, r_t, :], 0).astype(BF16)
    pad = ((0, 0), (0, page - t), (0, 0))
    kn = jnp.pad(ks32, pad)
    vn = jnp.pad(vs32, pad)
    rows = t * H_SB

    group = DECODE_PAGES_PER_STEP
    n_groups = n_pages // group
    assert n_groups * group == n_pages

    def cache_spec(g):
        def index(i, s, pt):
            return (pt[i, (n_groups - jnp.maximum(s, 1)) * group + g], 0, 0)
        return pl.BlockSpec((None, page * H_SB, DH_SB), index)

    per_b = lambda i, s, pt: (i, 0, 0)
    grid_spec = pltpu.PrefetchScalarGridSpec(
        num_scalar_prefetch=1,
        grid=(b, n_groups + 1),
        in_specs=[pl.BlockSpec((None, rows, dm), per_b)]
                 + [cache_spec(g) for g in range(group)] * 2
                 + [pl.BlockSpec((None, page, dm), per_b),
                    pl.BlockSpec((None, page, dm), per_b)],
        out_specs=pl.BlockSpec((None, t, dm), per_b),
        scratch_shapes=[pltpu.VMEM((rows, page), F32), pltpu.VMEM((rows, dm), F32)],
    )
    return pl.pallas_call(
        functools.partial(_sb_dec_kernel, past=past, n_groups=n_groups, group=group,
                          n_steps=t),
        grid_spec=grid_spec,
        out_shape=jax.ShapeDtypeStruct((b, t, dm), F32),
        compiler_params=_params("parallel", "arbitrary"),
        name="sb_attn_decode",
    )(page_table, qbd, *([kc] * group), *([vc] * group), kn, vn)


def _load_hist(buf_ref, hist_ref, nrows):
    if hist_ref is None:
        buf_ref[0:nrows, :] = jnp.zeros((nrows, buf_ref.shape[1]), F32)
    else:
        buf_ref[0:nrows, :] = hist_ref[...]


def _pool_kernel(*refs, stride, steps, pos0, has_hist):
    if has_hist:
        x_ref, hist_ref, g_ref, w_ref, sc_ref, o_ref, st_ref, buf_ref = refs
    else:
        x_ref, g_ref, w_ref, sc_ref, o_ref, st_ref, buf_ref = refs
        hist_ref = None
    j = pl.program_id(1)
    rows = steps * stride
    hpad = _hist_pad(POOL_HIST, stride)
    hrows = hpad * stride

    @pl.when(j == 0)
    def _():
        _load_hist(buf_ref, hist_ref, hrows)

    x = x_ref[...]
    u = _rms(x, g_ref[...])
    buf_ref[hrows:hrows + rows, :] = u
    step = lax.broadcasted_iota(jnp.int32, (rows, 1), 0) // stride
    pos = pos0 + j * steps + step
    for g, w in enumerate(POOL_WINDOWS):
        c0, c1 = g * POOL_GROUP, (g + 1) * POOL_GROUP
        ug = u[:, c0:c1]
        win = ug
        for i in range(1, w):
            win = win + buf_ref[(hpad - i) * stride:(hpad - i) * stride + rows, c0:c1]
        cnt = jnp.minimum(pos + 1, w).astype(F32)
        d = win / cnt - ug
        y = jnp.dot(d.astype(BF16), w_ref[g], preferred_element_type=F32) * sc_ref[:, c0:c1]
        o_ref[:, c0:c1] = x[:, c0:c1] + y
    last = buf_ref[rows:rows + hrows, :]
    buf_ref[0:hrows, :] = last
    st_ref[...] = last


def _pool_mixer(x3d, hist, g, w16, scale, *, stride, steps, pos0):
    gdim, total, d = x3d.shape
    rows = steps * stride
    nt = total // rows
    hrows = _hist_pad(POOL_HIST, stride) * stride
    tile = pl.BlockSpec((None, rows, d), lambda i, j: (i, j, 0))
    per_g = pl.BlockSpec((None, hrows, d), lambda i, j: (i, 0, 0))
    const2 = lambda i, j: (0, 0)
    in_specs = [tile] + ([per_g] if hist is not None else []) + [
        pl.BlockSpec((1, d), const2),
        pl.BlockSpec(w16.shape, lambda i, j: (0, 0, 0)),
        pl.BlockSpec((1, d), const2)]
    args = [x3d] + ([hist] if hist is not None else []) + [g.reshape(1, d), w16,
                                                          scale.reshape(1, d)]
    return pl.pallas_call(
        functools.partial(_pool_kernel, stride=stride, steps=steps, pos0=pos0,
                          has_hist=hist is not None),
        grid=(gdim, nt),
        in_specs=in_specs,
        out_specs=[tile, per_g],
        out_shape=[jax.ShapeDtypeStruct(x3d.shape, F32),
                   jax.ShapeDtypeStruct((gdim, hrows, d), F32)],
        scratch_shapes=[pltpu.VMEM((hrows + rows, d), F32)],
        compiler_params=_params("parallel", "arbitrary"),
        name="pool_mixer",
    )(*args)


def _conformer_kernel(*refs, stride, steps, has_hist):
    if has_hist:
        (x_ref, hist_ref, g_ref, w1_ref, b1_ref, wdw_ref, bdw_ref, lng_ref, lnb_ref,
         w2_ref, b2_ref, o_ref, st_ref, buf_ref, conv_ref, shift_ref) = refs
    else:
        (x_ref, g_ref, w1_ref, b1_ref, wdw_ref, bdw_ref, lng_ref, lnb_ref,
         w2_ref, b2_ref, o_ref, st_ref, buf_ref, conv_ref, shift_ref) = refs
        hist_ref = None
    j = pl.program_id(1)
    rows = steps * stride
    hpad = _hist_pad(CONV_WIDTH - 1, stride)
    hrows = hpad * stride
    d = x_ref.shape[-1]

    @pl.when(j == 0)
    def _():
        _load_hist(buf_ref, hist_ref, hrows)

    x = x_ref[...]
    u = _rms(x, g_ref[...]).astype(BF16)
    ag = jnp.dot(u, w1_ref[...], preferred_element_type=F32) + b1_ref[...]
    glu = ag[:, :d] * (1.0 / (1.0 + jnp.exp(-ag[:, d:])))
    buf_ref[hrows:hrows + rows, :] = glu
    offs = [(hpad - (CONV_WIDTH - 1) + k) * stride for k in range(CONV_WIDTH)]
    span = hrows + rows - SUBLANES
    for r in sorted({off % SUBLANES for off in offs} - {0}):
        shift_ref[r, 0:span, :] = buf_ref[r:r + span, :]
    for c0 in range(0, d, LANES):
        acc = jnp.zeros((rows, LANES), F32) + bdw_ref[:, c0:c0 + LANES]
        for k, off in enumerate(offs):
            r = off % SUBLANES
            src = buf_ref if r == 0 else shift_ref.at[r]
            acc = acc + wdw_ref[k:k + 1, c0:c0 + LANES] \
                * src[off - r:off - r + rows, c0:c0 + LANES]
        conv_ref[:, c0:c0 + LANES] = acc
    c = conv_ref[...]
    mu = jnp.mean(c, axis=-1, keepdims=True)
    cc = c - mu
    var = jnp.mean(cc * cc, axis=-1, keepdims=True)
    c = cc * lax.rsqrt(var + EPS) * lng_ref[...] + lnb_ref[...]
    c = c * (1.0 / (1.0 + jnp.exp(-c)))
    y = jnp.dot(c.astype(BF16), w2_ref[...], preferred_element_type=F32) + b2_ref[...]
    o_ref[...] = x + y
    last = buf_ref[rows:rows + hrows, :]
    buf_ref[0:hrows, :] = last
    st_ref[...] = last


def _conformer(x3d, hist, g, w1_16, b1, wdw, bdw, lng, lnb, w2_16, b2, *, stride, steps):
    gdim, total, d = x3d.shape
    rows = steps * stride
    nt = total // rows
    hrows = _hist_pad(CONV_WIDTH - 1, stride) * stride
    tile = pl.BlockSpec((None, rows, d), lambda i, j: (i, j, 0))
    per_g = pl.BlockSpec((None, hrows, d), lambda i, j: (i, 0, 0))
    const2 = lambda i, j: (0, 0)
    full = lambda a: pl.BlockSpec(a.shape, const2)
    wdw_p = jnp.pad(wdw, ((0, -CONV_WIDTH % SUBLANES), (0, 0)))
    params = [g.reshape(1, d), w1_16, b1.reshape(1, -1), wdw_p, bdw.reshape(1, d),
              lng.reshape(1, d), lnb.reshape(1, d), w2_16, b2.reshape(1, d)]
    in_specs = [tile] + ([per_g] if hist is not None else []) + [full(p) for p in params]
    args = [x3d] + ([hist] if hist is not None else []) + params
    return pl.pallas_call(
        functools.partial(_conformer_kernel, stride=stride, steps=steps,
                          has_hist=hist is not None),
        grid=(gdim, nt),
        in_specs=in_specs,
        out_specs=[tile, per_g],
        out_shape=[jax.ShapeDtypeStruct(x3d.shape, F32),
                   jax.ShapeDtypeStruct((gdim, hrows, d), F32)],
        scratch_shapes=[pltpu.VMEM((hrows + rows, d), F32), pltpu.VMEM((rows, d), F32),
                        pltpu.VMEM((SUBLANES if stride % SUBLANES else 1, hrows + rows, d),
                                   F32)],
        compiler_params=_params("parallel", "arbitrary"),
        name="conformer_conv",
    )(*args)


def _ffn_kernel(*refs, stride, steps, has_hist, final_norm):
    refs = list(refs)
    x_ref = refs.pop(0)
    hist_ref = refs.pop(0) if has_hist else None
    g_ref, wup_ref, wdw_ref, wdn_ref = refs[:4]
    refs = refs[4:]
    gf_ref = refs.pop(0) if final_norm else None
    o_ref, st_ref, buf_ref = refs
    j = pl.program_id(1)
    rows = steps * stride
    hrows = _hist_pad(FFN_CONV_WIDTH - 1, stride) * stride
    n_chunks, _, two_tc = wup_ref.shape
    tc = two_tc // 2

    @pl.when(j == 0)
    def _():
        if hist_ref is None:
            buf_ref[:, 0:hrows, :] = jnp.zeros((n_chunks, hrows, two_tc), F32)
        else:
            buf_ref[:, 0:hrows, :] = hist_ref[...]

    x = x_ref[...]
    h = _rms(x, g_ref[...]).astype(BF16)
    acc = None
    for c in range(n_chunks):
        buf = buf_ref.at[c]
        up = jnp.dot(h, wup_ref[c], preferred_element_type=F32)
        buf[hrows:hrows + rows, :] = up
        w = wdw_ref[c]
        back1 = buf[hrows - stride:hrows - stride + rows, :]
        back2 = buf[hrows - 2 * stride:hrows - 2 * stride + rows, :]
        conv = w[0:1] * back2 + w[1:2] * back1 + w[2:3] * up
        buf[0:hrows, :] = buf[rows:rows + hrows, :]
        gate = conv[:, :tc]
        act = gate * (1.0 / (1.0 + jnp.exp(-gate))) * conv[:, tc:]
        part = jnp.dot(act.astype(BF16), wdn_ref[c], preferred_element_type=F32)
        acc = part if acc is None else acc + part
    y = x + acc
    if final_norm:
        y = _rms(y, gf_ref[...])
    o_ref[...] = y

    @pl.when(j == pl.num_programs(1) - 1)
    def _():
        st_ref[...] = buf_ref[:, 0:hrows, :]


def _ffn(x3d, hist, g, wup_c, wdw_c, wdn_c, g_final, *, stride, steps):
    gdim, total, d = x3d.shape
    rows = steps * stride
    nt = total // rows
    hrows = _hist_pad(FFN_CONV_WIDTH - 1, stride) * stride
    n_chunks, _, two_tc = wup_c.shape
    tile = pl.BlockSpec((None, rows, d), lambda i, j: (i, j, 0))
    st_blk = pl.BlockSpec((None, n_chunks, hrows, two_tc), lambda i, j: (i, 0, 0, 0))
    const2 = lambda i, j: (0, 0)
    const3 = lambda i, j: (0, 0, 0)
    in_specs = [tile] + ([st_blk] if hist is not None else []) + [
        pl.BlockSpec((1, d), const2), pl.BlockSpec(wup_c.shape, const3),
        pl.BlockSpec(wdw_c.shape, const3), pl.BlockSpec(wdn_c.shape, const3)]
    args = [x3d] + ([hist] if hist is not None else []) + [g.reshape(1, d), wup_c, wdw_c, wdn_c]
    if g_final is not None:
        in_specs.append(pl.BlockSpec((1, d), const2))
        args.append(g_final.reshape(1, d))
    return pl.pallas_call(
        functools.partial(_ffn_kernel, stride=stride, steps=steps, has_hist=hist is not None,
                          final_norm=g_final is not None),
        grid=(gdim, nt),
        in_specs=in_specs,
        out_specs=[tile, st_blk],
        out_shape=[jax.ShapeDtypeStruct(x3d.shape, F32),
                   jax.ShapeDtypeStruct((gdim, n_chunks, hrows, two_tc), F32)],
        scratch_shapes=[pltpu.VMEM((n_chunks, hrows + rows, two_tc), F32)],
        compiler_params=_params("parallel", "arbitrary"),
        name="conv_ffn",
    )(*args)


def _ffn_weights(w_up, w_dw, w_down):
    d, two_f = w_up.shape
    f = two_f // 2
    tc = FFN_CHUNK
    nc = f // tc
    assert nc * tc == f
    wup_c = w_up.reshape(d, 2, nc, tc).transpose(2, 0, 1, 3).reshape(nc, d, 2 * tc).astype(BF16)
    wdw_c = w_dw.reshape(FFN_CONV_WIDTH, 2, nc, tc).transpose(2, 0, 1, 3).reshape(
        nc, FFN_CONV_WIDTH, 2 * tc)
    wdn_c = w_down.reshape(nc, tc, w_down.shape[1]).astype(BF16)
    return wup_c, wdw_c, wdn_c


def _ffn_state_from_chunks(st, stride):
    gdim, nc, _, two_tc = st.shape
    keep = (FFN_CONV_WIDTH - 1) * stride
    st = st[:, :, -keep:, :].reshape(gdim, nc, keep, 2, two_tc // 2)
    return st.transpose(0, 2, 3, 1, 4).reshape(gdim, keep, nc * two_tc)


def _ffn_hist_to_chunks(hist_rows, nc):
    r, two_f = hist_rows.shape
    stride = r // (FFN_CONV_WIDTH - 1)
    tc = two_f // 2 // nc
    hrows = _hist_pad(FFN_CONV_WIDTH - 1, stride) * stride
    h = jnp.pad(hist_rows, ((hrows - r, 0), (0, 0)))
    h = h.reshape(hrows, 2, nc, tc).transpose(2, 0, 1, 3)
    return h.reshape(1, nc, hrows, 2 * tc)


def _pad_hist(hist_rows, need, stride):
    extra = (_hist_pad(need, stride) - need) * stride
    return jnp.pad(hist_rows, ((extra, 0), (0, 0)))


def _time_major(a):
    a = jnp.swapaxes(a, 0, 1)
    return a.reshape(a.shape[0] * a.shape[1], *a.shape[2:])


def _batch_major(a, nb):
    a = a.reshape(a.shape[0] // nb, nb, *a.shape[1:])
    return jnp.swapaxes(a, 0, 1)


def kernel(x_prompt, x_sample, cache_k_diff, cache_v_diff, cache_k_sb, cache_v_sb, state_pool, state_conv, state_ffn, page_table, rel_bias, norm_mix, norm_ffn, norm_final, diff_w_qkv, diff_w_o, diff_lambda_q1, diff_lambda_k1, diff_lambda_q2, diff_lambda_k2, diff_subln, sb_w_qkv, sb_w_o, pool_w, pool_scale, conv_w_pw1, conv_b_pw1, conv_w_dw, conv_b_dw, conv_ln_g, conv_ln_b, conv_w_pw2, conv_b_pw2, ffn_w_up, ffn_w_dw, ffn_w_down):
    bp, lp, d = x_prompt.shape
    bs, ls, _ = x_sample.shape
    depth = ffn_w_up.shape[0]
    past = page_table.shape[1] * cache_k_diff.shape[1]
    seq_tile = min(lp, 256)

    xp = x_prompt
    xs = _time_major(x_sample)[None]
    ffn_p, ffn_s = [], []
    outs = {}
    for layer in range(depth):
        kind = layer % 4
        if kind == 0:
            lam_init = 0.8 - 0.6 * math.exp(-0.3 * layer)
            w16 = diff_w_qkv.astype(BF16)
            wo16 = diff_w_o.astype(BF16)
            lams = (diff_lambda_q1, diff_lambda_k1, diff_lambda_q2, diff_lambda_k2)
            wvt16 = diff_w_qkv[:, 2 * d:].T.astype(BF16)
            q16, k32, v32, k16, vt16 = _qkv_proj(xp.reshape(bp * lp, d), norm_mix[layer],
                                                 w16, wvt16)
            shp = (bp, lp, d)
            attn = _diff_attn_prompt(q16.reshape(shp), k16.reshape(shp), vt16,
                                     _bias_table(rel_bias), *lams, diff_subln, lam_init)
            xp = _mm_resid(attn.reshape(bp * lp, d), wo16, xp.reshape(bp * lp, d)).reshape(shp)
            outs['k_diff_p'] = k32.reshape(bp, lp, H_DIFF, 2 * DH_DIFF)
            outs['v_diff_p'] = v32.reshape(bp, lp, H_DIFF, 2 * DH_DIFF)

            q16, k32, v32 = _qkv_proj(xs[0], norm_mix[layer], w16)
            qs, ks, vs = (_batch_major(a, bs) for a in (q16, k32, v32))
            attn = _diff_attn_decode(qs, ks, vs, cache_k_diff, cache_v_diff, page_table,
                                     rel_bias, *lams, diff_subln, lam_init)
            xs = _mm_resid(_time_major(attn.reshape(bs, ls, d)), wo16, xs[0])[None]
            outs['k_diff_s'] = ks.reshape(bs, ls, H_DIFF, 2 * DH_DIFF)
            outs['v_diff_s'] = vs.reshape(bs, ls, H_DIFF, 2 * DH_DIFF)
        elif kind == 1:
            w16 = sb_w_qkv.astype(BF16)
            wo16 = sb_w_o.astype(BF16)
            wvt16 = sb_w_qkv[:, 2 * d:].T.astype(BF16)
            q16, k32, v32, k16, vt16 = _qkv_proj(xp.reshape(bp * lp, d), norm_mix[layer],
                                                 w16, wvt16)
            shp = (bp, lp, d)
            attn = _sb_attn_prompt(q16.reshape(shp), k16.reshape(shp), vt16)
            xp = _mm_resid(attn.reshape(bp * lp, d), wo16, xp.reshape(bp * lp, d)).reshape(shp)
            outs['k_sb_p'] = k32.reshape(bp, lp, H_SB, DH_SB)
            outs['v_sb_p'] = v32.reshape(bp, lp, H_SB, DH_SB)

            q16, k32, v32 = _qkv_proj(xs[0], norm_mix[layer], w16)
            qs, ks, vs = (_batch_major(a, bs) for a in (q16, k32, v32))
            attn = _sb_attn_decode(qs, ks, vs, cache_k_sb, cache_v_sb, page_table)
            xs = _mm_resid(_time_major(attn), wo16, xs[0])[None]
            outs['k_sb_s'] = ks.reshape(bs, ls, H_SB, DH_SB)
            outs['v_sb_s'] = vs.reshape(bs, ls, H_SB, DH_SB)
        elif kind == 2:
            pw16 = pool_w.astype(BF16)
            xp, st = _pool_mixer(xp, None, norm_mix[layer], pw16, pool_scale,
                                 stride=1, steps=seq_tile, pos0=0)
            outs['pool_p'] = st[:, -POOL_HIST:]
            hist = _pad_hist(_time_major(state_pool), POOL_HIST, bs)[None]
            xs, st = _pool_mixer(xs, hist, norm_mix[layer], pw16, pool_scale,
                                 stride=bs, steps=ls, pos0=past)
            outs['pool_s'] = _batch_major(st[0, -bs * POOL_HIST:], bs)
        else:
            w1 = conv_w_pw1.astype(BF16)
            w2 = conv_w_pw2.astype(BF16)
            cargs = (norm_mix[layer], w1, conv_b_pw1, conv_w_dw, conv_b_dw, conv_ln_g,
                     conv_ln_b, w2, conv_b_pw2)
            keep = CONV_WIDTH - 1
            xp, st = _conformer(xp, None, *cargs, stride=1, steps=seq_tile)
            outs['conv_p'] = st[:, -keep:]
            hist = _pad_hist(_time_major(state_conv), keep, bs)[None]
            xs, st = _conformer(xs, hist, *cargs, stride=bs, steps=ls)
            outs['conv_s'] = _batch_major(st[0, -bs * keep:], bs)

        wup_c, wdw_c, wdn_c = _ffn_weights(ffn_w_up[layer], ffn_w_dw[layer], ffn_w_down[layer])
        g_final = norm_final if layer == depth - 1 else None
        xp, st = _ffn(xp, None, norm_ffn[layer], wup_c, wdw_c, wdn_c, g_final,
                      stride=1, steps=seq_tile)
        ffn_p.append(_ffn_state_from_chunks(st, 1))
        hist = _ffn_hist_to_chunks(_time_major(state_ffn[layer]), wup_c.shape[0])
        xs, st = _ffn(xs, hist, norm_ffn[layer], wup_c, wdw_c, wdn_c, g_final,
                      stride=bs, steps=ls)
        ffn_s.append(_batch_major(_ffn_state_from_chunks(st, bs)[0], bs))

    y_prompt = xp
    y_sample = _batch_major(xs[0], bs)
    return (y_prompt, y_sample,
            outs['k_diff_p'], outs['v_diff_p'], outs['k_sb_p'], outs['v_sb_p'],
            outs['pool_p'], outs['conv_p'], jnp.stack(ffn_p, axis=0),
            outs['k_diff_s'], outs['v_diff_s'], outs['k_sb_s'], outs['v_sb_s'],
            outs['pool_s'], outs['conv_s'], jnp.stack(ffn_s, axis=0))
```

```python
import functools
import math

import jax
import jax.numpy as jnp
from jax import lax
from jax.experimental import pallas as pl
from jax.experimental.pallas import tpu as pltpu

D_MODEL = 1024
H_DIFF = 8
DH_DIFF = 64
H_SB = 16
DH_SB = 64
N_BUCKETS = 32
MAX_DISTANCE = 128
POOL_WINDOWS = (2, 4, 8, 16)
POOL_GROUP = D_MODEL // len(POOL_WINDOWS)
POOL_HIST = max(POOL_WINDOWS) - 1
CONV_WIDTH = 31
FFN_CONV_WIDTH = 3
EPS = 1e-6
NEG_INF = -1e30

BF16 = jnp.bfloat16
F32 = jnp.float32

V7X_VMEM_BYTES = 64 * 1024 * 1024
VMEM_LIMIT = V7X_VMEM_BYTES - 12 * 1024 * 1024
SUBLANES = 8
LANES = 128

BF16_ROWS = 16
ATT_BLOCK = 256
FFN_CHUNK = 256
DECODE_PAGES_PER_STEP = 4
SB_DEAD_LOG = -150.0


def _hist_pad(need, stride):
    h = need
    while (h * stride) % SUBLANES:
        h += 1
    return h

_NT = (((1,), (1,)), ((), ()))


def _params(*sem):
    return pltpu.CompilerParams(dimension_semantics=sem, vmem_limit_bytes=VMEM_LIMIT)


def _rms(x, g):
    return x * lax.rsqrt(jnp.mean(x * x, axis=-1, keepdims=True) + EPS) * g


def _row_tile(m, want):
    t = min(m, want)
    assert m % t == 0, (m, t)
    return t


def _qkv_kernel(x_ref, g_ref, w_ref, *refs, with_vt):
    if with_vt:
        wvt_ref, q16_ref, k32_ref, v32_ref, k16_ref, vt16_ref = refs
    else:
        q16_ref, k32_ref, v32_ref = refs
    n = q16_ref.shape[-1]
    u = _rms(x_ref[...], g_ref[...]).astype(BF16)
    q = jnp.dot(u, w_ref[:, 0:n], preferred_element_type=F32)
    q16_ref[...] = q.astype(BF16)
    k = jnp.dot(u, w_ref[:, n:2 * n], preferred_element_type=F32)
    k32_ref[...] = k
    v32_ref[...] = jnp.dot(u, w_ref[:, 2 * n:3 * n], preferred_element_type=F32)
    if with_vt:
        k16_ref[...] = k.astype(BF16)
        vt = lax.dot_general(wvt_ref[...], u, _NT, preferred_element_type=F32)
        vt16_ref[...] = vt.astype(BF16)


def _qkv_proj(x2d, g, w16, wvt16=None):
    m, d = x2d.shape
    n = w16.shape[1] // 3
    tm = _row_tile(m, 256)
    row = lambda i: (i, 0)
    const = lambda i: (0, 0)
    out_blk = pl.BlockSpec((tm, n), row)
    in_specs = [pl.BlockSpec((tm, d), row), pl.BlockSpec((1, d), const),
                pl.BlockSpec((d, 3 * n), const)]
    out_specs = [out_blk] * 3
    out_shape = [jax.ShapeDtypeStruct((m, n), BF16), jax.ShapeDtypeStruct((m, n), F32),
                 jax.ShapeDtypeStruct((m, n), F32)]
    args = [x2d, g.reshape(1, d), w16]
    if wvt16 is not None:
        in_specs.append(pl.BlockSpec((n, d), const))
        out_specs += [out_blk, pl.BlockSpec((n, tm), lambda i: (0, i))]
        out_shape += [jax.ShapeDtypeStruct((m, n), BF16), jax.ShapeDtypeStruct((n, m), BF16)]
        args.append(wvt16)
    return pl.pallas_call(
        functools.partial(_qkv_kernel, with_vt=wvt16 is not None),
        grid=(m // tm,),
        in_specs=in_specs,
        out_specs=out_specs,
        out_shape=out_shape,
        compiler_params=_params("parallel"),
        name="qkv_proj",
    )(*args)


def _mm_resid_kernel(a_ref, w_ref, r_ref, o_ref):
    o_ref[...] = r_ref[...] + jnp.dot(a_ref[...].astype(BF16), w_ref[...],
                                      preferred_element_type=F32)


def _mm_resid(a2d, w16, resid2d):
    m, k = a2d.shape
    n = w16.shape[1]
    tm = _row_tile(m, 512)
    row = lambda i: (i, 0)
    return pl.pallas_call(
        _mm_resid_kernel,
        grid=(m // tm,),
        in_specs=[pl.BlockSpec((tm, k), row), pl.BlockSpec((k, n), lambda i: (0, 0)),
                  pl.BlockSpec((tm, n), row)],
        out_specs=pl.BlockSpec((tm, n), row),
        out_shape=jax.ShapeDtypeStruct((m, n), F32),
        compiler_params=_params("parallel"),
        name="out_proj_resid",
    )(a2d, w16, resid2d)


def _t5_bucket(rel):
    n = jnp.maximum(rel, 0)
    max_exact = N_BUCKETS // 2
    nf = jnp.maximum(n, 1).astype(F32)
    large = max_exact + (jnp.log(nf / max_exact) / math.log(MAX_DISTANCE / max_exact)
                         * (N_BUCKETS - max_exact)).astype(jnp.int32)
    large = jnp.minimum(large, N_BUCKETS - 1)
    return jnp.where(n < max_exact, n, large)


def _bias_table_kernel(rb_ref, o_ref):
    d = pl.program_id(0)
    shape = (ATT_BLOCK, ATT_BLOCK)
    rel = d * ATT_BLOCK + lax.broadcasted_iota(jnp.int32, shape, 1) \
        - lax.broadcasted_iota(jnp.int32, shape, 0)
    bucket = _t5_bucket(rel)
    for h in range(H_DIFF):
        val = jnp.zeros(shape, F32)
        for b in range(N_BUCKETS):
            val = jnp.where(bucket == b, rb_ref[b, h], val)
        o_ref[h] = jnp.where(rel >= 0, val, NEG_INF)


def _bias_table(rel_bias):
    assert ATT_BLOCK >= MAX_DISTANCE
    return pl.pallas_call(
        _bias_table_kernel,
        grid=(3,),
        in_specs=[pl.BlockSpec(memory_space=pltpu.SMEM)],
        out_specs=pl.BlockSpec((None, H_DIFF, ATT_BLOCK, ATT_BLOCK), lambda d: (d, 0, 0, 0)),
        out_shape=jax.ShapeDtypeStruct((3, H_DIFF, ATT_BLOCK, ATT_BLOCK), F32),
        compiler_params=_params("parallel"),
        name="t5_bias_table",
    )(rel_bias)


def _diff_lambda(lq1, lk1, lq2, lk2, lam_init):
    e1 = jnp.exp(jnp.sum(lq1 * lk1, axis=-1, keepdims=True))
    e2 = jnp.exp(jnp.sum(lq2 * lk2, axis=-1, keepdims=True))
    return e1 - e2 + lam_init


def _load_vt_chunks(vt_ref, vx_ref, ones_rows):
    blk = ATT_BLOCK
    hd = vt_ref.shape[0]
    for j in range(vx_ref.shape[0]):
        vx_ref[j, 0:hd, :] = vt_ref[:, j * blk:(j + 1) * blk]
        if ones_rows:
            vx_ref[j, hd:hd + ones_rows, :] = jnp.ones((ones_rows, blk), BF16)


def _diff_attn_kernel(q_ref, k_ref, vt_ref, bias_ref, lq1_ref, lk1_ref, lq2_ref, lk2_ref,
                      subln_ref, o_ref, vx_ref, *, lam_init):
    blk = ATT_BLOCK
    hd = 2 * DH_DIFF
    nq = q_ref.shape[0] // blk
    first = lax.broadcasted_iota(jnp.int32, (blk, hd), 1) < DH_DIFF
    lam = _diff_lambda(lq1_ref[...], lk1_ref[...], lq2_ref[...], lk2_ref[...], lam_init)
    subln = subln_ref[...] * (1.0 - lam_init)
    scale = DH_DIFF ** -0.5
    _load_vt_chunks(vt_ref, vx_ref, BF16_ROWS)

    def q_block(qi, _):
        q = q_ref[pl.ds(pl.multiple_of(qi * blk, blk), blk), :] * scale
        zero = jnp.zeros_like(q)
        qa = jnp.where(first, q, zero)
        qb = jnp.where(first, zero, q)

        def scores(kj):
            ks = k_ref[pl.ds(pl.multiple_of(kj * blk, blk), blk), :]
            bias = bias_ref[jnp.minimum(qi - kj, 2)]
            return (lax.dot_general(ks, qa, _NT, preferred_element_type=F32) + bias,
                    lax.dot_general(ks, qb, _NT, preferred_element_type=F32) + bias)

        def update(kj, s, m, a):
            mn = jnp.maximum(m, jnp.max(s, axis=0, keepdims=True))
            alpha = jnp.exp(m - mn)
            p = jnp.exp(s - mn)
            a = alpha * a + jnp.dot(vx_ref[kj], p.astype(BF16), preferred_element_type=F32)
            return mn, a

        def kv_block(kj, carry):
            s1, s2, m1, a1, m2, a2 = carry
            n1, n2 = scores(kj + 1)
            m1, a1 = update(kj, s1, m1, a1)
            m2, a2 = update(kj, s2, m2, a2)
            return n1, n2, m1, a1, m2, a2

        m0 = jnp.full((1, blk), NEG_INF, F32)
        a0 = jnp.zeros((hd + BF16_ROWS, blk), F32)
        s1, s2, m1, a1, m2, a2 = lax.fori_loop(0, qi, kv_block, scores(0) + (m0, a0, m0, a0))
        m1, a1 = update(qi, s1, m1, a1)
        m2, a2 = update(qi, s2, m2, a2)
        o = a1[:hd] * (1.0 / a1[hd:hd + 1]) - lam * (a2[:hd] * (1.0 / a2[hd:hd + 1]))
        o = o * lax.rsqrt(jnp.mean(o * o, axis=0, keepdims=True) + EPS)
        o_ref[pl.ds(pl.multiple_of(qi * blk, blk), blk), :] = (o.T * subln).astype(o_ref.dtype)
        return 0

    lax.fori_loop(0, nq, q_block, 0)


def _diff_attn_prompt(q16, k16, vt16, bias_tab, lq1, lk1, lq2, lk2, subln, lam_init):
    b, l, _ = q16.shape
    hd = 2 * DH_DIFF
    seq = pl.BlockSpec((None, l, hd), lambda i, h: (i, 0, h))
    vec = lambda n: pl.BlockSpec((1, n), lambda i, h: (0, 0))
    return pl.pallas_call(
        functools.partial(_diff_attn_kernel, lam_init=lam_init),
        grid=(b, H_DIFF),
        in_specs=[seq, seq, pl.BlockSpec((hd, l), lambda i, h: (h, i)),
                  pl.BlockSpec((3, None, ATT_BLOCK, ATT_BLOCK), lambda i, h: (0, h, 0, 0)),
                  vec(DH_DIFF), vec(DH_DIFF), vec(DH_DIFF), vec(DH_DIFF), vec(hd)],
        out_specs=seq,
        out_shape=jax.ShapeDtypeStruct((b, l, H_DIFF * hd), BF16),
        scratch_shapes=[pltpu.VMEM((l // ATT_BLOCK, hd + BF16_ROWS, ATT_BLOCK), BF16)],
        compiler_params=_params("parallel", "parallel"),
        name="diff_attn_prompt",
    )(q16, k16, vt16, bias_tab, lq1.reshape(1, -1), lk1.reshape(1, -1), lq2.reshape(1, -1),
      lk2.reshape(1, -1), subln.reshape(1, -1))


def _diff_dec_kernel(pt_ref, qm_ref, *refs, past, n_groups, group, n_steps, lam_init):
    kc_refs, vc_refs = refs[:group], refs[group:2 * group]
    (kn_ref, vn_ref, tab_ref, lq1_ref, lk1_ref, lq2_ref, lk2_ref, subln_ref, o_ref,
     m_ref, l_ref, acc_ref, far_ref) = refs[2 * group:]
    s_id = pl.program_id(1)
    page = kc_refs[0].shape[0]
    rows = qm_ref.shape[0]
    hd = 2 * DH_DIFF
    last_bucket = tab_ref[:, N_BUCKETS - 1:N_BUCKETS]

    def own_head(shape):
        return (lax.broadcasted_iota(jnp.int32, shape, 1) & (H_DIFF - 1)) \
            == (lax.broadcasted_iota(jnp.int32, shape, 0) & (H_DIFF - 1))

    @pl.when(s_id == 0)
    def _():
        m_ref[...] = jnp.full(m_ref.shape, NEG_INF, F32)
        l_ref[...] = jnp.zeros(l_ref.shape, F32)
        acc_ref[...] = jnp.zeros(acc_ref.shape, F32)
        far_ref[...] = jnp.where(own_head(far_ref.shape), last_bucket, NEG_INF)

    def near_term(keys, kpos0):
        shape = (rows, keys * H_DIFF)
        step = (lax.broadcasted_iota(jnp.int32, shape, 0) >> 3) & (n_steps - 1)
        rel = (past + step) - (kpos0 + (lax.broadcasted_iota(jnp.int32, shape, 1) >> 3))
        bucket = _t5_bucket(rel)
        bias = jnp.zeros(shape, F32)
        for b in range(N_BUCKETS):
            bias = jnp.where(bucket == b, tab_ref[:, b:b + 1], bias)
        return jnp.where(jnp.logical_and(own_head(shape), rel >= 0), bias, NEG_INF)

    def process(blocks):
        scores, values = [], []
        for k_ref, v_ref, term in blocks:
            keys = k_ref.shape[0]
            kb = k_ref[...].reshape(keys * H_DIFF, hd).astype(BF16)
            values.append(v_ref[...].reshape(keys * H_DIFF, hd).astype(BF16))
            scores.append(lax.dot_general(qm_ref[...], kb, _NT, preferred_element_type=F32)
                          + term)
        m = m_ref[...]
        mn = m
        for s in scores:
            mn = jnp.maximum(mn, jnp.max(s, axis=-1, keepdims=True))
        alpha = jnp.exp(m - mn)
        l = alpha * l_ref[...]
        acc = alpha * acc_ref[...]
        for s, vb in zip(scores, values):
            p = jnp.exp(s - mn)
            l = l + jnp.sum(p, axis=-1, keepdims=True)
            acc = acc + jnp.dot(p.astype(BF16), vb, preferred_element_type=F32)
        l_ref[...] = l
        acc_ref[...] = acc
        m_ref[...] = mn

    @pl.when(s_id < n_groups)
    def _():
        blocks = []
        for g in range(group):
            kpos0 = (s_id * group + g) * page
            far = kpos0 + page + MAX_DISTANCE <= past
            term = lax.cond(far, lambda: far_ref[...],
                            functools.partial(near_term, page, kpos0))
            blocks.append((kc_refs[g], vc_refs[g], term))
        process(blocks)

    @pl.when(s_id == n_groups)
    def _():
        process([(kn_ref, vn_ref, near_term(kn_ref.shape[0], past))])
        lam = _diff_lambda(lq1_ref[...], lk1_ref[...], lq2_ref[...], lk2_ref[...], lam_init)
        subln = subln_ref[...] * (1.0 - lam_init)
        accn = acc_ref[...] * (1.0 / l_ref[...])
        half = n_steps * H_DIFF
        o = accn[:half] - lam * accn[half:]
        o_ref[...] = _rms(o, subln).reshape(o_ref.shape)


def _diff_attn_decode(qs16, ks32, vs32, cache_k, cache_v, page_table, rel_bias,
                      lq1, lk1, lq2, lk2, subln, lam_init):
    b, t, dm = qs16.shape
    n_phys, page = cache_k.shape[:2]
    n_pages = page_table.shape[1]
    past = n_pages * page
    hd = 2 * DH_DIFF
    assert t & (t - 1) == 0 and t <= SUBLANES and cache_k.shape[2:] == (H_DIFF, hd)
    half_c = (jnp.arange(hd) // DH_DIFF)[None, :] == jnp.arange(2)[:, None]
    q4 = (qs16 * (DH_DIFF ** -0.5)).reshape(b, 1, t, H_DIFF, hd)
    qm = jnp.where(half_c[None, :, None, None, :], q4, 0).astype(BF16)
    rows = 2 * t * H_DIFF
    qm = qm.reshape(b, rows, hd)
    pad = ((0, 0), (0, SUBLANES - t), (0, 0), (0, 0))
    kn = jnp.pad(ks32.reshape(b, t, H_DIFF, hd), pad)
    vn = jnp.pad(vs32.reshape(b, t, H_DIFF, hd), pad)
    tab = jnp.tile(rel_bias.T, (2 * t, 1))

    group = DECODE_PAGES_PER_STEP
    n_groups = n_pages // group
    assert n_groups * group == n_pages

    def cache_spec(g):
        def index(i, s, pt):
            return (pt[i, jnp.minimum(s, n_groups - 1) * group + g], 0, 0, 0)
        return pl.BlockSpec((None, page, H_DIFF, hd), index)

    per_b3 = lambda i, s, pt: (i, 0, 0)
    per_b4 = lambda i, s, pt: (i, 0, 0, 0)
    vec = lambda n: pl.BlockSpec((1, n), lambda i, s, pt: (0, 0))
    grid_spec = pltpu.PrefetchScalarGridSpec(
        num_scalar_prefetch=1,
        grid=(b, n_groups + 1),
        in_specs=[pl.BlockSpec((None, rows, hd), per_b3)]
                 + [cache_spec(g) for g in range(group)] * 2
                 + [pl.BlockSpec((None, SUBLANES, H_DIFF, hd), per_b4),
                  pl.BlockSpec((None, SUBLANES, H_DIFF, hd), per_b4),
                  pl.BlockSpec((rows, N_BUCKETS), lambda i, s, pt: (0, 0)),
                  vec(DH_DIFF), vec(DH_DIFF), vec(DH_DIFF), vec(DH_DIFF), vec(hd)],
        out_specs=pl.BlockSpec((None, t, H_DIFF, hd), per_b4),
        scratch_shapes=[pltpu.VMEM((rows, 1), F32), pltpu.VMEM((rows, 1), F32),
                        pltpu.VMEM((rows, hd), F32),
                        pltpu.VMEM((rows, page * H_DIFF), F32)],
    )
    return pl.pallas_call(
        functools.partial(_diff_dec_kernel, past=past, n_groups=n_groups, group=group,
                          n_steps=t, lam_init=lam_init),
        grid_spec=grid_spec,
        out_shape=jax.ShapeDtypeStruct((b, t, H_DIFF, hd), F32),
        compiler_params=_params("parallel", "arbitrary"),
        name="diff_attn_decode",
    )(page_table, qm, *([cache_k] * group), *([cache_v] * group), kn, vn, tab,
      lq1.reshape(1, -1), lk1.reshape(1, -1), lq2.reshape(1, -1), lk2.reshape(1, -1),
      subln.reshape(1, -1))


def _suffix_matrix(n):
    r = lax.broadcasted_iota(jnp.int32, (n, 2 * n), 0)
    c = lax.broadcasted_iota(jnp.int32, (n, 2 * n), 1)
    return jnp.where(jnp.logical_or(c >= n, r > c), 1.0, 0.0).astype(BF16)


def _sb_weights(z, carry, suffix, mask):
    n = z.shape[1]
    e = jnp.exp(-jnp.abs(z))
    ls = jnp.minimum(z, 0.0) - jnp.log(1.0 + e)
    lk = ls - z
    if mask is not None:
        lk = jnp.where(mask, lk, 0.0)
    hi = lk.astype(BF16)
    lo = (lk - hi.astype(F32)).astype(BF16)
    t = jnp.dot(jnp.concatenate([hi, lo], axis=0), suffix, preferred_element_type=F32)
    t = t[:z.shape[0]] + t[z.shape[0]:]
    a = jnp.exp(ls + t[:, :n] + carry)
    if mask is not None:
        a = jnp.where(mask, a, 0.0)
    return a, carry + t[:, n:]


def _sb_weights_t(z, carry, suffix_t, mask):
    n = z.shape[0]
    e = jnp.exp(-jnp.abs(z))
    ls = jnp.minimum(z, 0.0) - jnp.log(1.0 + e)
    lk = ls - z
    if mask is not None:
        lk = jnp.where(mask, lk, 0.0)
    t = jnp.dot(suffix_t, lk.astype(BF16), preferred_element_type=F32)
    a = jnp.exp(ls + t[:n] + carry)
    if mask is not None:
        a = jnp.where(mask, a, 0.0)
    return a, carry + t[n:n + 1]


def _sb_attn_kernel(q_ref, k_ref, vt_ref, o_ref, vx_ref):
    blk = ATT_BLOCK
    nq = q_ref.shape[0] // blk
    first = lax.broadcasted_iota(jnp.int32, (blk, 2 * DH_SB), 1) < DH_SB
    rows = lax.broadcasted_iota(jnp.int32, (blk + BF16_ROWS, blk), 0)
    cols = lax.broadcasted_iota(jnp.int32, (blk + BF16_ROWS, blk), 1)
    suffix_t = jnp.where(jnp.logical_or(rows >= blk, cols > rows), 1.0, 0.0).astype(BF16)
    strict = lax.broadcasted_iota(jnp.int32, (blk, blk), 0) \
        < lax.broadcasted_iota(jnp.int32, (blk, blk), 1)
    upper = lax.broadcasted_iota(jnp.int32, (2 * DH_SB, blk), 0) < DH_SB
    scale = DH_SB ** -0.5
    _load_vt_chunks(vt_ref, vx_ref, 0)

    def q_block(qi, _):
        q = q_ref[pl.ds(pl.multiple_of(qi * blk, blk), blk), :] * scale
        zero = jnp.zeros_like(q)
        qa = jnp.where(first, q, zero)
        qb = jnp.where(first, zero, q)

        def scores(kj):
            ks = k_ref[pl.ds(pl.multiple_of(kj * blk, blk), blk), :]
            return (lax.dot_general(ks, qa, _NT, preferred_element_type=F32),
                    lax.dot_general(ks, qb, _NT, preferred_element_type=F32))

        def update(kj, za, zb, state, mask):
            ca, acc_a, cb, acc_b = state
            vx = vx_ref[kj]
            wa, ca = _sb_weights_t(za, ca, suffix_t, mask)
            acc_a = acc_a + jnp.dot(vx, wa.astype(BF16), preferred_element_type=F32)
            wb, cb = _sb_weights_t(zb, cb, suffix_t, mask)
            acc_b = acc_b + jnp.dot(vx, wb.astype(BF16), preferred_element_type=F32)
            return ca, acc_a, cb, acc_b

        def alive(state):
            return jnp.maximum(jnp.max(state[0]), jnp.max(state[2])) > SB_DEAD_LOG

        def kv_block(loop):
            i, carry = loop[0], loop[2:]
            kj = qi - 1 - i
            nxt = scores(jnp.maximum(kj - 1, 0))
            state = update(kj, carry[0], carry[1], carry[2:], None)
            return (i + 1, alive(state)) + nxt + state

        c0 = jnp.zeros((1, blk), F32)
        a0 = jnp.zeros((2 * DH_SB, blk), F32)
        nxt = scores(jnp.maximum(qi - 1, 0))
        state = update(qi, *scores(qi), (c0, a0, c0, a0), strict)
        loop = lax.while_loop(lambda c: jnp.logical_and(c[0] < qi, c[1]), kv_block,
                              (jnp.int32(0), alive(state)) + nxt + state)
        _, _, _, _, _, acc_a, _, acc_b = loop
        o = jnp.where(upper, acc_a, acc_b)
        o_ref[pl.ds(pl.multiple_of(qi * blk, blk), blk), :] = o.T.astype(o_ref.dtype)
        return 0

    lax.fori_loop(0, nq, q_block, 0)


def _sb_attn_prompt(q16, k16, vt16):
    b, l, dm = q16.shape
    seq = pl.BlockSpec((None, l, 2 * DH_SB), lambda i, h: (i, 0, h))
    return pl.pallas_call(
        _sb_attn_kernel,
        grid=(b, H_SB // 2),
        in_specs=[seq, seq, pl.BlockSpec((2 * DH_SB, l), lambda i, h: (h, i))],
        out_specs=seq,
        out_shape=jax.ShapeDtypeStruct((b, l, dm), BF16),
        scratch_shapes=[pltpu.VMEM((l // ATT_BLOCK, 2 * DH_SB, ATT_BLOCK), BF16)],
        compiler_params=_params("parallel", "parallel"),
        name="sb_attn_prompt",
    )(q16, k16, vt16)


def _sb_dec_kernel(pt_ref, qbd_ref, *refs, past, n_groups, group, n_steps):
    kc_refs, vc_refs = refs[:group], refs[group:2 * group]
    kn_ref, vn_ref, o_ref, carry_ref, acc_ref = refs[2 * group:]
    s_id = pl.program_id(1)
    page = kn_ref.shape[0]
    rows = qbd_ref.shape[0]
    suffix = _suffix_matrix(page)

    def dense(ref):
        return ref[...].astype(BF16)

    def from_cache(ref):
        heads = [ref[:, h, :] for h in range(H_SB)]
        return jnp.concatenate(heads, axis=-1).astype(BF16)

    def process(blocks, load, masked):
        mask = None
        if masked:
            shape = (rows, page)
            qpos = past + (lax.broadcasted_iota(jnp.int32, shape, 0) >> 4)
            mask = past + lax.broadcasted_iota(jnp.int32, shape, 1) < qpos
        carry = carry_ref[...]
        acc = acc_ref[...]
        for k_ref, v_ref in blocks:
            z = lax.dot_general(qbd_ref[...], load(k_ref), _NT, preferred_element_type=F32)
            w, carry = _sb_weights(z, carry, suffix, mask)
            acc = acc + jnp.dot(w.astype(BF16), load(v_ref), preferred_element_type=F32)
        carry_ref[...] = carry
        acc_ref[...] = acc

    @pl.when(s_id == 0)
    def _():
        carry_ref[...] = jnp.zeros(carry_ref.shape, F32)
        acc_ref[...] = jnp.zeros(acc_ref.shape, F32)
        process([(kn_ref, vn_ref)], dense, masked=True)

    @pl.when(s_id > 0)
    def _():
        @pl.when(jnp.max(carry_ref[...]) > SB_DEAD_LOG)
        def _():
            process([(kc_refs[g], vc_refs[g]) for g in reversed(range(group))], from_cache,
                    masked=False)

    @pl.when(s_id == n_groups)
    def _():
        acc = acc_ref[...]
        shape = (H_SB, H_SB * DH_SB)
        own = lax.broadcasted_iota(jnp.int32, shape, 1) // DH_SB \
            == lax.broadcasted_iota(jnp.int32, shape, 0)
        for t in range(n_steps):
            o_ref[t:t + 1, :] = jnp.sum(jnp.where(own, acc[t * H_SB:(t + 1) * H_SB], 0.0),
                                        axis=0, keepdims=True)


def _sb_attn_decode(qs16, ks32, vs32, cache_k, cache_v, page_table):
    b, t, dm = qs16.shape
    n_phys, page = cache_k.shape[:2]
    n_pages = page_table.shape[1]
    past = n_pages * page
    assert t * H_SB == 64
    kc, vc = cache_k, cache_v
    col_head = jnp.arange(dm) // DH_SB
    r = jnp.arange(t * H_SB)
    r_t, r_h = r // H_SB, r % H_SB
    sel = col_head[None, :] == r_h[:, None]
    qbd = jnp.where(sel[None], (qs16 * (DH_SB ** -0.5))[:, r_t, :], 0).astype(BF16)
    pad = ((0, 0), (0, page - t), (0, 0))
    kn = jnp.pad(ks32, pad)
    vn = jnp.pad(vs32, pad)
    rows = t * H_SB

    group = DECODE_PAGES_PER_STEP
    n_groups = n_pages // group
    assert n_groups * group == n_pages

    def cache_spec(g):
        def index(i, s, pt):
            return (pt[i, (n_groups - jnp.maximum(s, 1)) * group + g], 0, 0, 0)
        return pl.BlockSpec((None, page, H_SB, DH_SB), index)

    per_b = lambda i, s, pt: (i, 0, 0)
    grid_spec = pltpu.PrefetchScalarGridSpec(
        num_scalar_prefetch=1,
        grid=(b, n_groups + 1),
        in_specs=[pl.BlockSpec((None, rows, dm), per_b)]
                 + [cache_spec(g) for g in range(group)] * 2
                 + [pl.BlockSpec((None, page, dm), per_b),
                    pl.BlockSpec((None, page, dm), per_b)],
        out_specs=pl.BlockSpec((None, t, dm), per_b),
        scratch_shapes=[pltpu.VMEM((rows, page), F32), pltpu.VMEM((rows, dm), F32)],
    )
    return pl.pallas_call(
        functools.partial(_sb_dec_kernel, past=past, n_groups=n_groups, group=group,
                          n_steps=t),
        grid_spec=grid_spec,
        out_shape=jax.ShapeDtypeStruct((b, t, dm), F32),
        compiler_params=_params("parallel", "arbitrary"),
        name="sb_attn_decode",
    )(page_table, qbd, *([kc] * group), *([vc] * group), kn, vn)


def _load_hist(buf_ref, hist_ref, nrows):
    if hist_ref is None:
        buf_ref[0:nrows, :] = jnp.zeros((nrows, buf_ref.shape[1]), F32)
    else:
        buf_ref[0:nrows, :] = hist_ref[...]


def _pool_kernel(*refs, stride, steps, pos0, has_hist):
    if has_hist:
        x_ref, hist_ref, g_ref, w_ref, sc_ref, o_ref, st_ref, buf_ref = refs
    else:
        x_ref, g_ref, w_ref, sc_ref, o_ref, st_ref, buf_ref = refs
        hist_ref = None
    j = pl.program_id(1)
    rows = steps * stride
    hpad = _hist_pad(POOL_HIST, stride)
    hrows = hpad * stride

    @pl.when(j == 0)
    def _():
        _load_hist(buf_ref, hist_ref, hrows)

    x = x_ref[...]
    u = _rms(x, g_ref[...])
    buf_ref[hrows:hrows + rows, :] = u
    step = lax.broadcasted_iota(jnp.int32, (rows, 1), 0) // stride
    pos = pos0 + j * steps + step
    for g, w in enumerate(POOL_WINDOWS):
        c0, c1 = g * POOL_GROUP, (g + 1) * POOL_GROUP
        ug = u[:, c0:c1]
        win = ug
        for i in range(1, w):
            win = win + buf_ref[(hpad - i) * stride:(hpad - i) * stride + rows, c0:c1]
        cnt = jnp.minimum(pos + 1, w).astype(F32)
        d = win / cnt - ug
        y = jnp.dot(d.astype(BF16), w_ref[g], preferred_element_type=F32) * sc_ref[:, c0:c1]
        o_ref[:, c0:c1] = x[:, c0:c1] + y
    last = buf_ref[rows:rows + hrows, :]
    buf_ref[0:hrows, :] = last
    st_ref[...] = last


def _pool_mixer(x3d, hist, g, w16, scale, *, stride, steps, pos0):
    gdim, total, d = x3d.shape
    rows = steps * stride
    nt = total // rows
    hrows = _hist_pad(POOL_HIST, stride) * stride
    tile = pl.BlockSpec((None, rows, d), lambda i, j: (i, j, 0))
    per_g = pl.BlockSpec((None, hrows, d), lambda i, j: (i, 0, 0))
    const2 = lambda i, j: (0, 0)
    in_specs = [tile] + ([per_g] if hist is not None else []) + [
        pl.BlockSpec((1, d), const2),
        pl.BlockSpec(w16.shape, lambda i, j: (0, 0, 0)),
        pl.BlockSpec((1, d), const2)]
    args = [x3d] + ([hist] if hist is not None else []) + [g.reshape(1, d), w16,
                                                          scale.reshape(1, d)]
    return pl.pallas_call(
        functools.partial(_pool_kernel, stride=stride, steps=steps, pos0=pos0,
                          has_hist=hist is not None),
        grid=(gdim, nt),
        in_specs=in_specs,
        out_specs=[tile, per_g],
        out_shape=[jax.ShapeDtypeStruct(x3d.shape, F32),
                   jax.ShapeDtypeStruct((gdim, hrows, d), F32)],
        scratch_shapes=[pltpu.VMEM((hrows + rows, d), F32)],
        compiler_params=_params("parallel", "arbitrary"),
        name="pool_mixer",
    )(*args)


def _conformer_kernel(*refs, stride, steps, has_hist):
    if has_hist:
        (x_ref, hist_ref, g_ref, w1_ref, b1_ref, wdw_ref, bdw_ref, lng_ref, lnb_ref,
         w2_ref, b2_ref, o_ref, st_ref, buf_ref, conv_ref, shift_ref) = refs
    else:
        (x_ref, g_ref, w1_ref, b1_ref, wdw_ref, bdw_ref, lng_ref, lnb_ref,
         w2_ref, b2_ref, o_ref, st_ref, buf_ref, conv_ref, shift_ref) = refs
        hist_ref = None
    j = pl.program_id(1)
    rows = steps * stride
    hpad = _hist_pad(CONV_WIDTH - 1, stride)
    hrows = hpad * stride
    d = x_ref.shape[-1]

    @pl.when(j == 0)
    def _():
        _load_hist(buf_ref, hist_ref, hrows)

    x = x_ref[...]
    u = _rms(x, g_ref[...]).astype(BF16)
    ag = jnp.dot(u, w1_ref[...], preferred_element_type=F32) + b1_ref[...]
    glu = ag[:, :d] * (1.0 / (1.0 + jnp.exp(-ag[:, d:])))
    buf_ref[hrows:hrows + rows, :] = glu
    offs = [(hpad - (CONV_WIDTH - 1) + k) * stride for k in range(CONV_WIDTH)]
    span = hrows + rows - SUBLANES
    for r in sorted({off % SUBLANES for off in offs} - {0}):
        shift_ref[r, 0:span, :] = buf_ref[r:r + span, :]
    for c0 in range(0, d, LANES):
        acc = jnp.zeros((rows, LANES), F32) + bdw_ref[:, c0:c0 + LANES]
        for k, off in enumerate(offs):
            r = off % SUBLANES
            src = buf_ref if r == 0 else shift_ref.at[r]
            acc = acc + wdw_ref[k:k + 1, c0:c0 + LANES] \
                * src[off - r:off - r + rows, c0:c0 + LANES]
        conv_ref[:, c0:c0 + LANES] = acc
    c = conv_ref[...]
    mu = jnp.mean(c, axis=-1, keepdims=True)
    cc = c - mu
    var = jnp.mean(cc * cc, axis=-1, keepdims=True)
    c = cc * lax.rsqrt(var + EPS) * lng_ref[...] + lnb_ref[...]
    c = c * (1.0 / (1.0 + jnp.exp(-c)))
    y = jnp.dot(c.astype(BF16), w2_ref[...], preferred_element_type=F32) + b2_ref[...]
    o_ref[...] = x + y
    last = buf_ref[rows:rows + hrows, :]
    buf_ref[0:hrows, :] = last
    st_ref[...] = last


def _conformer(x3d, hist, g, w1_16, b1, wdw, bdw, lng, lnb, w2_16, b2, *, stride, steps):
    gdim, total, d = x3d.shape
    rows = steps * stride
    nt = total // rows
    hrows = _hist_pad(CONV_WIDTH - 1, stride) * stride
    tile = pl.BlockSpec((None, rows, d), lambda i, j: (i, j, 0))
    per_g = pl.BlockSpec((None, hrows, d), lambda i, j: (i, 0, 0))
    const2 = lambda i, j: (0, 0)
    full = lambda a: pl.BlockSpec(a.shape, const2)
    wdw_p = jnp.pad(wdw, ((0, -CONV_WIDTH % SUBLANES), (0, 0)))
    params = [g.reshape(1, d), w1_16, b1.reshape(1, -1), wdw_p, bdw.reshape(1, d),
              lng.reshape(1, d), lnb.reshape(1, d), w2_16, b2.reshape(1, d)]
    in_specs = [tile] + ([per_g] if hist is not None else []) + [full(p) for p in params]
    args = [x3d] + ([hist] if hist is not None else []) + params
    return pl.pallas_call(
        functools.partial(_conformer_kernel, stride=stride, steps=steps,
                          has_hist=hist is not None),
        grid=(gdim, nt),
        in_specs=in_specs,
        out_specs=[tile, per_g],
        out_shape=[jax.ShapeDtypeStruct(x3d.shape, F32),
                   jax.ShapeDtypeStruct((gdim, hrows, d), F32)],
        scratch_shapes=[pltpu.VMEM((hrows + rows, d), F32), pltpu.VMEM((rows, d), F32),
                        pltpu.VMEM((SUBLANES if stride % SUBLANES else 1, hrows + rows, d),
                                   F32)],
        compiler_params=_params("parallel", "arbitrary"),
        name="conformer_conv",
    )(*args)


def _ffn_kernel(*refs, stride, steps, has_hist, final_norm):
    refs = list(refs)
    x_ref = refs.pop(0)
    hist_ref = refs.pop(0) if has_hist else None
    g_ref, wup_ref, wdw_ref, wdn_ref = refs[:4]
    refs = refs[4:]
    gf_ref = refs.pop(0) if final_norm else None
    o_ref, st_ref, buf_ref = refs
    j = pl.program_id(1)
    rows = steps * stride
    hrows = _hist_pad(FFN_CONV_WIDTH - 1, stride) * stride
    n_chunks, _, two_tc = wup_ref.shape
    tc = two_tc // 2

    @pl.when(j == 0)
    def _():
        if hist_ref is None:
            buf_ref[:, 0:hrows, :] = jnp.zeros((n_chunks, hrows, two_tc), F32)
        else:
            buf_ref[:, 0:hrows, :] = hist_ref[...]

    x = x_ref[...]
    h = _rms(x, g_ref[...]).astype(BF16)
    acc = None
    for c in range(n_chunks):
        buf = buf_ref.at[c]
        up = jnp.dot(h, wup_ref[c], preferred_element_type=F32)
        buf[hrows:hrows + rows, :] = up
        w = wdw_ref[c]
        back1 = buf[hrows - stride:hrows - stride + rows, :]
        back2 = buf[hrows - 2 * stride:hrows - 2 * stride + rows, :]
        conv = w[0:1] * back2 + w[1:2] * back1 + w[2:3] * up
        buf[0:hrows, :] = buf[rows:rows + hrows, :]
        gate = conv[:, :tc]
        act = gate * (1.0 / (1.0 + jnp.exp(-gate))) * conv[:, tc:]
        part = jnp.dot(act.astype(BF16), wdn_ref[c], preferred_element_type=F32)
        acc = part if acc is None else acc + part
    y = x + acc
    if final_norm:
        y = _rms(y, gf_ref[...])
    o_ref[...] = y

    @pl.when(j == pl.num_programs(1) - 1)
    def _():
        st_ref[...] = buf_ref[:, 0:hrows, :]


def _ffn(x3d, hist, g, wup_c, wdw_c, wdn_c, g_final, *, stride, steps):
    gdim, total, d = x3d.shape
    rows = steps * stride
    nt = total // rows
    hrows = _hist_pad(FFN_CONV_WIDTH - 1, stride) * stride
    n_chunks, _, two_tc = wup_c.shape
    tile = pl.BlockSpec((None, rows, d), lambda i, j: (i, j, 0))
    st_blk = pl.BlockSpec((None, n_chunks, hrows, two_tc), lambda i, j: (i, 0, 0, 0))
    const2 = lambda i, j: (0, 0)
    const3 = lambda i, j: (0, 0, 0)
    in_specs = [tile] + ([st_blk] if hist is not None else []) + [
        pl.BlockSpec((1, d), const2), pl.BlockSpec(wup_c.shape, const3),
        pl.BlockSpec(wdw_c.shape, const3), pl.BlockSpec(wdn_c.shape, const3)]
    args = [x3d] + ([hist] if hist is not None else []) + [g.reshape(1, d), wup_c, wdw_c, wdn_c]
    if g_final is not None:
        in_specs.append(pl.BlockSpec((1, d), const2))
        args.append(g_final.reshape(1, d))
    return pl.pallas_call(
        functools.partial(_ffn_kernel, stride=stride, steps=steps, has_hist=hist is not None,
                          final_norm=g_final is not None),
        grid=(gdim, nt),
        in_specs=in_specs,
        out_specs=[tile, st_blk],
        out_shape=[jax.ShapeDtypeStruct(x3d.shape, F32),
                   jax.ShapeDtypeStruct((gdim, n_chunks, hrows, two_tc), F32)],
        scratch_shapes=[pltpu.VMEM((n_chunks, hrows + rows, two_tc), F32)],
        compiler_params=_params("parallel", "arbitrary"),
        name="conv_ffn",
    )(*args)


def _ffn_weights(w_up, w_dw, w_down):
    d, two_f = w_up.shape
    f = two_f // 2
    tc = FFN_CHUNK
    nc = f // tc
    assert nc * tc == f
    wup_c = w_up.reshape(d, 2, nc, tc).transpose(2, 0, 1, 3).reshape(nc, d, 2 * tc).astype(BF16)
    wdw_c = w_dw.reshape(FFN_CONV_WIDTH, 2, nc, tc).transpose(2, 0, 1, 3).reshape(
        nc, FFN_CONV_WIDTH, 2 * tc)
    wdn_c = w_down.reshape(nc, tc, w_down.shape[1]).astype(BF16)
    return wup_c, wdw_c, wdn_c


def _ffn_state_from_chunks(st, stride):
    gdim, nc, _, two_tc = st.shape
    keep = (FFN_CONV_WIDTH - 1) * stride
    st = st[:, :, -keep:, :].reshape(gdim, nc, keep, 2, two_tc // 2)
    return st.transpose(0, 2, 3, 1, 4).reshape(gdim, keep, nc * two_tc)


def _ffn_hist_to_chunks(hist_rows, nc):
    r, two_f = hist_rows.shape
    stride = r // (FFN_CONV_WIDTH - 1)
    tc = two_f // 2 // nc
    hrows = _hist_pad(FFN_CONV_WIDTH - 1, stride) * stride
    h = jnp.pad(hist_rows, ((hrows - r, 0), (0, 0)))
    h = h.reshape(hrows, 2, nc, tc).transpose(2, 0, 1, 3)
    return h.reshape(1, nc, hrows, 2 * tc)


def _pad_hist(hist_rows, need, stride):
    extra = (_hist_pad(need, stride) - need) * stride
    return jnp.pad(hist_rows, ((extra, 0), (0, 0)))


def _time_major(a):
    a = jnp.swapaxes(a, 0, 1)
    return a.reshape(a.shape[0] * a.shape[1], *a.shape[2:])


def _batch_major(a, nb):
    a = a.reshape(a.shape[0] // nb, nb, *a.shape[1:])
    return jnp.swapaxes(a, 0, 1)


def kernel(x_prompt, x_sample, cache_k_diff, cache_v_diff, cache_k_sb, cache_v_sb, state_pool, state_conv, state_ffn, page_table, rel_bias, norm_mix, norm_ffn, norm_final, diff_w_qkv, diff_w_o, diff_lambda_q1, diff_lambda_k1, diff_lambda_q2, diff_lambda_k2, diff_subln, sb_w_qkv, sb_w_o, pool_w, pool_scale, conv_w_pw1, conv_b_pw1, conv_w_dw, conv_b_dw, conv_ln_g, conv_ln_b, conv_w_pw2, conv_b_pw2, ffn_w_up, ffn_w_dw, ffn_w_down):
    bp, lp, d = x_prompt.shape
    bs, ls, _ = x_sample.shape
    depth = ffn_w_up.shape[0]
    past = page_table.shape[1] * cache_k_diff.shape[1]
    seq_tile = min(lp, 256)

    xp = x_prompt
    xs = _time_major(x_sample)[None]
    ffn_p, ffn_s = [], []
    outs = {}
    for layer in range(depth):
        kind = layer % 4
        if kind == 0:
            lam_init = 0.8 - 0.6 * math.exp(-0.3 * layer)
            w16 = diff_w_qkv.astype(BF16)
            wo16 = diff_w_o.astype(BF16)
            lams = (diff_lambda_q1, diff_lambda_k1, diff_lambda_q2, diff_lambda_k2)
            wvt16 = diff_w_qkv[:, 2 * d:].T.astype(BF16)
            q16, k32, v32, k16, vt16 = _qkv_proj(xp.reshape(bp * lp, d), norm_mix[layer],
                                                 w16, wvt16)
            shp = (bp, lp, d)
            attn = _diff_attn_prompt(q16.reshape(shp), k16.reshape(shp), vt16,
                                     _bias_table(rel_bias), *lams, diff_subln, lam_init)
            xp = _mm_resid(attn.reshape(bp * lp, d), wo16, xp.reshape(bp * lp, d)).reshape(shp)
            outs['k_diff_p'] = k32.reshape(bp, lp, H_DIFF, 2 * DH_DIFF)
            outs['v_diff_p'] = v32.reshape(bp, lp, H_DIFF, 2 * DH_DIFF)

            q16, k32, v32 = _qkv_proj(xs[0], norm_mix[layer], w16)
            qs, ks, vs = (_batch_major(a, bs) for a in (q16, k32, v32))
            attn = _diff_attn_decode(qs, ks, vs, cache_k_diff, cache_v_diff, page_table,
                                     rel_bias, *lams, diff_subln, lam_init)
            xs = _mm_resid(_time_major(attn.reshape(bs, ls, d)), wo16, xs[0])[None]
            outs['k_diff_s'] = ks.reshape(bs, ls, H_DIFF, 2 * DH_DIFF)
            outs['v_diff_s'] = vs.reshape(bs, ls, H_DIFF, 2 * DH_DIFF)
        elif kind == 1:
            w16 = sb_w_qkv.astype(BF16)
            wo16 = sb_w_o.astype(BF16)
            wvt16 = sb_w_qkv[:, 2 * d:].T.astype(BF16)
            q16, k32, v32, k16, vt16 = _qkv_proj(xp.reshape(bp * lp, d), norm_mix[layer],
                                                 w16, wvt16)
            shp = (bp, lp, d)
            attn = _sb_attn_prompt(q16.reshape(shp), k16.reshape(shp), vt16)
            xp = _mm_resid(attn.reshape(bp * lp, d), wo16, xp.reshape(bp * lp, d)).reshape(shp)
            outs['k_sb_p'] = k32.reshape(bp, lp, H_SB, DH_SB)
            outs['v_sb_p'] = v32.reshape(bp, lp, H_SB, DH_SB)

            q16, k32, v32 = _qkv_proj(xs[0], norm_mix[layer], w16)
            qs, ks, vs = (_batch_major(a, bs) for a in (q16, k32, v32))
            attn = _sb_attn_decode(qs, ks, vs, cache_k_sb, cache_v_sb, page_table)
            xs = _mm_resid(_time_major(attn), wo16, xs[0])[None]
            outs['k_sb_s'] = ks.reshape(bs, ls, H_SB, DH_SB)
            outs['v_sb_s'] = vs.reshape(bs, ls, H_SB, DH_SB)
        elif kind == 2:
            pw16 = pool_w.astype(BF16)
            xp, st = _pool_mixer(xp, None, norm_mix[layer], pw16, pool_scale,
                                 stride=1, steps=seq_tile, pos0=0)
            outs['pool_p'] = st[:, -POOL_HIST:]
            hist = _pad_hist(_time_major(state_pool), POOL_HIST, bs)[None]
            xs, st = _pool_mixer(xs, hist, norm_mix[layer], pw16, pool_scale,
                                 stride=bs, steps=ls, pos0=past)
            outs['pool_s'] = _batch_major(st[0, -bs * POOL_HIST:], bs)
        else:
            w1 = conv_w_pw1.astype(BF16)
            w2 = conv_w_pw2.astype(BF16)
            cargs = (norm_mix[layer], w1, conv_b_pw1, conv_w_dw, conv_b_dw, conv_ln_g,
                     conv_ln_b, w2, conv_b_pw2)
            keep = CONV_WIDTH - 1
            xp, st = _conformer(xp, None, *cargs, stride=1, steps=seq_tile)
            outs['conv_p'] = st[:, -keep:]
            hist = _pad_hist(_time_major(state_conv), keep, bs)[None]
            xs, st = _conformer(xs, hist, *cargs, stride=bs, steps=ls)
            outs['conv_s'] = _batch_major(st[0, -bs * keep:], bs)

        wup_c, wdw_c, wdn_c = _ffn_weights(ffn_w_up[layer], ffn_w_dw[layer], ffn_w_down[layer])
        g_final = norm_final if layer == depth - 1 else None
        xp, st = _ffn(xp, None, norm_ffn[layer], wup_c, wdw_c, wdn_c, g_final,
                      stride=1, steps=seq_tile)
        ffn_p.append(_ffn_state_from_chunks(st, 1))
        hist = _ffn_hist_to_chunks(_time_major(state_ffn[layer]), wup_c.shape[0])
        xs, st = _ffn(xs, hist, norm_ffn[layer], wup_c, wdw_c, wdn_c, g_final,
                      stride=bs, steps=ls)
        ffn_s.append(_batch_major(_ffn_state_from_chunks(st, bs)[0], bs))

    y_prompt = xp
    y_sample = _batch_major(xs[0], bs)
    return (y_prompt, y_sample,
            outs['k_diff_p'], outs['v_diff_p'], outs['k_sb_p'], outs['v_sb_p'],
            outs['pool_p'], outs['conv_p'], jnp.stack(ffn_p, axis=0),
            outs['k_diff_s'], outs['v_diff_s'], outs['k_sb_s'], outs['v_sb_s'],
            outs['pool_s'], outs['conv_s'], jnp.stack(ffn_s, axis=0))
```
